```python
import math
import jax, jax.numpy as jnp
from jax import lax
import numpy as np

D_MODEL = 4096
BATCH = 2
SEQ = 8192
DEPTH = 2

HEAD_DIM = 64
N_Q_HEADS = D_MODEL // 2 // HEAD_DIM
N_KV_HEADS = N_Q_HEADS // 8
GROUP = N_Q_HEADS // N_KV_HEADS
WINDOW = 128
BLOCK = 128
ATTN_W = N_Q_HEADS * HEAD_DIM
KV_W = N_KV_HEADS * HEAD_DIM
CONF_W = D_MODEL // 4
CONF_K = 31
SC_W = D_MODEL // 4
SC_K = 3
N_BRANCH = 3
D_FF = ((8 * D_MODEL // 3 + 255) // 256) * 256
ALPHA = (2 * DEPTH) ** 0.25
BETA = (8 * DEPTH) ** -0.25
LN_EPS = 1e-5
NEG_INF = -1e30

SPLIT_SIZES = (ATTN_W, KV_W, KV_W, 2 * CONF_W, 3 * SC_W, N_BRANCH * D_MODEL)
SPLIT_IDX = tuple(int(v) for v in np.cumsum(SPLIT_SIZES)[:-1])
IN_W = int(sum(SPLIT_SIZES))

kernel_name = "hybrid_swa_conformer_shortconv_deepnorm"


def layer_norm(x, g, b):
    xf = x.astype(jnp.float32)
    mu = jnp.mean(xf, axis=-1, keepdims=True)
    var = jnp.mean(jnp.square(xf - mu), axis=-1, keepdims=True)
    y = (xf - mu) * lax.rsqrt(var + LN_EPS)
    return (y * g.astype(jnp.float32) + b.astype(jnp.float32)).astype(x.dtype)


def causal_depthwise_conv(u, w):
    K, C = w.shape
    return lax.conv_general_dilated(
        u, w[:, None, :].astype(u.dtype), window_strides=(1,), padding=[(K - 1, 0)],
        dimension_numbers=("NWC", "WIO", "NWC"), feature_group_count=C)


def sliding_window_gqa_sinks(q, k, v, sinks):
    B, T, _ = q.shape
    nb = T // BLOCK
    qb = q.reshape(B, nb, BLOCK, N_KV_HEADS, GROUP, HEAD_DIM)

    def band_blocks(t):
        tp = jnp.pad(t.reshape(B, T, N_KV_HEADS, HEAD_DIM), ((0, 0), (BLOCK, 0), (0, 0), (0, 0)))
        tp = tp.reshape(B, nb + 1, BLOCK, N_KV_HEADS, HEAD_DIM)
        return jnp.concatenate([tp[:, :-1], tp[:, 1:]], axis=2)

    kb = band_blocks(k)
    vb = band_blocks(v)
    s = jnp.einsum("bnqhgd,bnkhd->bnhgqk", qb, kb,
                   preferred_element_type=jnp.float32) * (HEAD_DIM ** -0.5)
    qi = jnp.arange(BLOCK)[:, None]
    kj = jnp.arange(2 * BLOCK)[None, :]
    rel = qi + BLOCK - kj
    band = (rel >= 0) & (rel < WINDOW)
    key_pos = jnp.arange(nb)[:, None] * BLOCK + kj - BLOCK
    mask = band[None] & (key_pos >= 0)[:, None, :]
    s = jnp.where(mask[None, :, None, None], s, NEG_INF)
    sink = sinks.astype(jnp.float32).reshape(1, 1, N_KV_HEADS, GROUP, 1, 1)
    m = jnp.maximum(jnp.max(s, axis=-1, keepdims=True), sink)
    p = jnp.exp(s - m)
    denom = jnp.sum(p, axis=-1, keepdims=True) + jnp.exp(sink - m)
    w = (p / denom).astype(v.dtype)
    o = jnp.einsum("bnhgqk,bnkhd->bnqhgd", w, vb)
    return o.reshape(B, T, ATTN_W)


def hybrid_layer(x, w_in, sinks, conf_dw, conf_dw_b, conf_ln_g, conf_ln_b, sc_dw,
                 w_proj_attn, w_proj_conf, w_proj_sc, w_out, ln1_g, ln1_b,
                 w_ffn_in, w_ffn_down, ln2_g, ln2_b):
    B, T, _ = x.shape
    z = x @ w_in
    q, k, v, conf_u, sc_u, gate_u = jnp.split(z, SPLIT_IDX, axis=-1)

    o_a = sliding_window_gqa_sinks(q, k, v, sinks)

    c_val, c_gate = jnp.split(conf_u, 2, axis=-1)
    c = c_val * jax.nn.sigmoid(c_gate)
    c = causal_depthwise_conv(c, conf_dw) + conf_dw_b
    o_b = jax.nn.silu(layer_norm(c, conf_ln_g, conf_ln_b))

    g_b, g_c, x_in = jnp.split(sc_u, 3, axis=-1)
    o_c = g_b * causal_depthwise_conv(g_c * x_in, sc_dw)

    gates = jax.nn.sigmoid(gate_u).reshape(B, T, N_BRANCH, D_MODEL)
    merged = (gates[:, :, 0] * (o_a @ w_proj_attn)
              + gates[:, :, 1] * (o_b @ w_proj_conf)
              + gates[:, :, 2] * (o_c @ w_proj_sc))
    h = layer_norm(ALPHA * x + merged @ w_out, ln1_g, ln1_b)

    f_gate, f_up = jnp.split(h @ w_ffn_in, 2, axis=-1)
    ffn = (jax.nn.silu(f_gate) * f_up) @ w_ffn_down
    return layer_norm(ALPHA * h + ffn, ln2_g, ln2_b)


def setup_inputs(seed: int = 0) -> dict:
    key = jax.random.key(seed)
    ks = jax.random.split(key, 32)
    f32 = jnp.float32

    def dense(k, shape, fan_in, scale=1.0):
        return jax.random.normal(k, shape, f32) * (fan_in ** -0.5) * scale

    x = jax.random.normal(ks[0], (BATCH, SEQ, D_MODEL), f32)
    w_in = jnp.concatenate([
        dense(ks[1], (DEPTH, D_MODEL, ATTN_W), D_MODEL),
        dense(ks[2], (DEPTH, D_MODEL, KV_W), D_MODEL),
        dense(ks[3], (DEPTH, D_MODEL, KV_W), D_MODEL, BETA),
        dense(ks[4], (DEPTH, D_MODEL, 2 * CONF_W), D_MODEL),
        dense(ks[5], (DEPTH, D_MODEL, 3 * SC_W), D_MODEL),
        dense(ks[6], (DEPTH, D_MODEL, N_BRANCH * D_MODEL), D_MODEL),
    ], axis=-1)
    attn_sinks = jax.random.normal(ks[7], (DEPTH, N_Q_HEADS), f32)
    conf_dw = dense(ks[8], (DEPTH, CONF_K, CONF_W), CONF_K)
    conf_dw_b = 0.02 * jax.random.normal(ks[9], (DEPTH, CONF_W), f32)
    conf_ln_g = 1.0 + 0.02 * jax.random.normal(ks[10], (DEPTH, CONF_W), f32)
    conf_ln_b = 0.02 * jax.random.normal(ks[11], (DEPTH, CONF_W), f32)
    sc_dw = dense(ks[12], (DEPTH, SC_K, SC_W), SC_K)
    w_proj_attn = dense(ks[13], (DEPTH, ATTN_W, D_MODEL), ATTN_W, BETA)
    w_proj_conf = dense(ks[14], (DEPTH, CONF_W, D_MODEL), CONF_W, BETA)
    w_proj_sc = dense(ks[15], (DEPTH, SC_W, D_MODEL), SC_W, BETA)
    w_out = dense(ks[16], (DEPTH, D_MODEL, D_MODEL), D_MODEL, BETA)
    ln1_g = 1.0 + 0.02 * jax.random.normal(ks[17], (DEPTH, D_MODEL), f32)
    ln1_b = 0.02 * jax.random.normal(ks[18], (DEPTH, D_MODEL), f32)
    w_ffn_in = dense(ks[19], (DEPTH, D_MODEL, 2 * D_FF), D_MODEL, BETA)
    w_ffn_down = dense(ks[20], (DEPTH, D_FF, D_MODEL), D_FF, BETA)
    ln2_g = 1.0 + 0.02 * jax.random.normal(ks[21], (DEPTH, D_MODEL), f32)
    ln2_b = 0.02 * jax.random.normal(ks[22], (DEPTH, D_MODEL), f32)
    return {"x": x, "w_in": w_in, "attn_sinks": attn_sinks, "conf_dw": conf_dw,
            "conf_dw_b": conf_dw_b, "conf_ln_g": conf_ln_g, "conf_ln_b": conf_ln_b,
            "sc_dw": sc_dw, "w_proj_attn": w_proj_attn, "w_proj_conf": w_proj_conf,
            "w_proj_sc": w_proj_sc, "w_out": w_out, "ln1_g": ln1_g, "ln1_b": ln1_b,
            "w_ffn_in": w_ffn_in, "w_ffn_down": w_ffn_down, "ln2_g": ln2_g, "ln2_b": ln2_b}


def reference(x, w_in, attn_sinks, conf_dw, conf_dw_b, conf_ln_g, conf_ln_b, sc_dw,
              w_proj_attn, w_proj_conf, w_proj_sc, w_out, ln1_g, ln1_b,
              w_ffn_in, w_ffn_down, ln2_g, ln2_b):
    h = x
    for l in range(DEPTH):
        h = hybrid_layer(h, w_in[l], attn_sinks[l], conf_dw[l], conf_dw_b[l], conf_ln_g[l],
                         conf_ln_b[l], sc_dw[l], w_proj_attn[l], w_proj_conf[l], w_proj_sc[l],
                         w_out[l], ln1_g[l], ln1_b[l], w_ffn_in[l], w_ffn_down[l],
                         ln2_g[l], ln2_b[l])
    return h
```

```python
import functools

import jax
import jax.numpy as jnp
from jax import lax
from jax.experimental import pallas as pl
from jax.experimental.pallas import tpu as pltpu

D_MODEL = 4096
DEPTH = 2
HEAD_DIM = 64
N_Q_HEADS = 32
N_KV_HEADS = 4
GROUP = 8
BLOCK = 128
ATTN_W = N_Q_HEADS * HEAD_DIM
KV_W = N_KV_HEADS * HEAD_DIM
CONF_W = 1024
CONF_K = 31
SC_W = 1024
SC_K = 3
D_FF = 11008
ALPHA = (2 * DEPTH) ** 0.25
LN_EPS = 1e-5
NEG_INF = -1e30

OFF_Q = 0
OFF_K = ATTN_W
OFF_V = ATTN_W + KV_W
OFF_CONF_VAL = ATTN_W + 2 * KV_W
OFF_CONF_GATE = OFF_CONF_VAL + CONF_W
OFF_SC_B = OFF_CONF_GATE + CONF_W
OFF_SC_C = OFF_SC_B + SC_W
OFF_SC_X = OFF_SC_C + SC_W
OFF_GATE = OFF_SC_X + SC_W
QKV_W = ATTN_W + 2 * KV_W

V7X_VMEM_BYTES = 64 * 1024 * 1024
V7X_VMEM_CAP = V7X_VMEM_BYTES - 6 * 1024 * 1024
LANE = 128

D_FF_PAD = 11264

F32 = jnp.float32
BF16 = jnp.bfloat16


def _compiler_params(n_grid, vmem_estimate):
    limit = min(V7X_VMEM_CAP, int(vmem_estimate * 1.2) + (8 << 20))
    return pltpu.CompilerParams(
        dimension_semantics=("arbitrary",) * n_grid, vmem_limit_bytes=limit)


def _nbytes(shape, dtype):
    n = 1
    for s in shape:
        n *= s
    return n * jnp.dtype(dtype).itemsize


def _proj_body(*refs, n_dots, epilogue):
    x_ref = refs[0]
    w_refs = refs[1:1 + n_dots]
    o_refs = refs[1 + n_dots:]
    x = x_ref[...]
    zs = [jnp.dot(x, w_ref[...], preferred_element_type=F32) for w_ref in w_refs]
    outs = epilogue(*zs)
    for o_ref, o in zip(o_refs, outs):
        o_ref[...] = o.astype(o_ref.dtype)


def _col_index(m, n, *, base):
    return (0, base + n)


def _proj(lhs, w, col_offsets, epilogue, out_dtypes, *, n_cols, bm, bn, name):
    M, K = lhs.shape
    in_specs = [pl.BlockSpec((bm, K), lambda m, n: (m, 0))]
    for off in col_offsets:
        assert off % bn == 0
        in_specs.append(pl.BlockSpec((K, bn), functools.partial(_col_index, base=off // bn)))
    out_specs = [pl.BlockSpec((bm, bn), lambda m, n: (m, n)) for _ in out_dtypes]
    out_shape = [jax.ShapeDtypeStruct((M, n_cols), dt) for dt in out_dtypes]
    est = 2 * (_nbytes((bm, K), lhs.dtype) + len(col_offsets) * _nbytes((K, bn), w.dtype)
               + sum(_nbytes((bm, bn), dt) for dt in out_dtypes))
    est += (len(col_offsets) + len(out_dtypes)) * _nbytes((bm, bn), F32)
    return pl.pallas_call(
        functools.partial(_proj_body, n_dots=len(col_offsets), epilogue=epilogue),
        grid=(M // bm, n_cols // bn),
        in_specs=in_specs, out_specs=out_specs, out_shape=out_shape,
        compiler_params=_compiler_params(2, est), name=name,
    )(lhs, *([w] * len(col_offsets)))


def _qkv_epilogue(z):
    return (z,)


def _mixer_epilogue(c_val, c_gate, g_b, g_c, x_in):
    return (c_val * jax.nn.sigmoid(c_gate), g_b, g_c * x_in)


def _swiglu_epilogue(f_gate, f_up):
    return (jax.nn.silu(f_gate) * f_up,)


def _attn_body(sink_ref, q_ref, kp_ref, kc_ref, vp_ref, vc_ref, bias_ref, o_ref):
    bias = bias_ref[0]
    lo = lax.broadcasted_iota(jnp.int32, (2 * BLOCK, LANE), 1) < HEAD_DIM
    lo_out = lax.broadcasted_iota(jnp.int32, (4 * BLOCK, LANE), 1) < HEAD_DIM
    scale = HEAD_DIM ** -0.5
    for j in range(N_KV_HEADS // 2):
        lanes = slice(j * LANE, (j + 1) * LANE)
        kcat = jnp.concatenate([kp_ref[:, lanes], kc_ref[:, lanes]], axis=0).astype(F32)
        vcat = jnp.concatenate([vp_ref[:, lanes], vc_ref[:, lanes]], axis=0).astype(F32)
        krot = pltpu.roll(kcat, HEAD_DIM, axis=1)
        vrot = pltpu.roll(vcat, HEAD_DIM, axis=1)
        for hh in range(2):
            h = 2 * j + hh
            k_lo, k_hi = (kcat, krot) if hh == 0 else (krot, kcat)
            v_lo, v_hi = (vcat, vrot) if hh == 0 else (vrot, vcat)
            kk = jnp.concatenate([jnp.where(lo, k_lo, 0.0), jnp.where(lo, 0.0, k_hi)],
                                 axis=0).astype(BF16)
            vv = jnp.concatenate([jnp.where(lo, v_lo, 0.0), jnp.where(lo, 0.0, v_hi)],
                                 axis=0).astype(BF16)
            base = h * GROUP * HEAD_DIM
            qs = jnp.concatenate(
                [q_ref[:, base + p * LANE: base + (p + 1) * LANE] for p in range(GROUP // 2)],
                axis=0)
            s = lax.dot_general(qs, kk, (((1,), (1,)), ((), ())),
                                preferred_element_type=F32)
            s = s * scale + bias
            sink_e = [jnp.concatenate(
                [jnp.full((BLOCK, 1), sink_ref[h * GROUP + 2 * p + e], F32)
                 for p in range(GROUP // 2)], axis=0) for e in range(2)]
            ps, dens = [], []
            for e in range(2):
                se = s[:, e * 2 * BLOCK:(e + 1) * 2 * BLOCK]
                m = jnp.maximum(jnp.max(se, axis=-1, keepdims=True), sink_e[e])
                pe = jnp.exp(se - m)
                dens.append(jnp.sum(pe, axis=-1, keepdims=True) + jnp.exp(sink_e[e] - m))
                ps.append(pe.astype(BF16))
            pcat = jnp.concatenate(ps, axis=1)
            o = jnp.dot(pcat, vv, preferred_element_type=F32)
            o = o / jnp.where(lo_out, dens[0], dens[1])
            for p in range(GROUP // 2):
                o_ref[:, base + p * LANE: base + (p + 1) * LANE] = (
                    o[p * BLOCK:(p + 1) * BLOCK].astype(o_ref.dtype))


def _attn_bias():
    r = jnp.arange(4 * BLOCK)[:, None] % BLOCK
    c = jnp.arange(4 * BLOCK)[None, :] % (2 * BLOCK)
    band = (c - r >= 1) & (c - r <= BLOCK)
    first = band & (c >= BLOCK)
    return jnp.where(jnp.stack([band, first]), 0.0, NEG_INF).astype(F32)


def _attention(qkv, sinks, seq_len):
    M = qkv.shape[0]
    blocks_per_seq = seq_len // BLOCK
    k_col = OFF_K // KV_W
    v_col = OFF_V // KV_W

    def prev(i):
        return jnp.maximum(i - 1, 0)

    in_specs = [
        pl.BlockSpec(memory_space=pltpu.SMEM),
        pl.BlockSpec((BLOCK, ATTN_W), lambda i: (i, 0)),
        pl.BlockSpec((BLOCK, KV_W), lambda i: (prev(i), k_col)),
        pl.BlockSpec((BLOCK, KV_W), lambda i: (i, k_col)),
        pl.BlockSpec((BLOCK, KV_W), lambda i: (prev(i), v_col)),
        pl.BlockSpec((BLOCK, KV_W), lambda i: (i, v_col)),
        pl.BlockSpec((1, 4 * BLOCK, 4 * BLOCK),
                     lambda i: (jnp.where(i % blocks_per_seq == 0, 1, 0), 0, 0)),
    ]
    est = 2 * (2 * _nbytes((BLOCK, ATTN_W), BF16) + 4 * _nbytes((BLOCK, KV_W), BF16)
               + _nbytes((4 * BLOCK, 4 * BLOCK), F32)) + 16 * _nbytes((4 * BLOCK, 4 * BLOCK), F32)
    return pl.pallas_call(
        _attn_body,
        grid=(M // BLOCK,),
        in_specs=in_specs,
        out_specs=pl.BlockSpec((BLOCK, ATTN_W), lambda i: (i, 0)),
        out_shape=jax.ShapeDtypeStruct((M, ATTN_W), BF16),
        compiler_params=_compiler_params(1, est), name="swa_attention",
    )(sinks, qkv, qkv, qkv, qkv, qkv, _attn_bias())


CONV_BT = 256
CONF_HALO = 32
SC_HALO = 8
CONV_ROWS = 64
CONV_LANES = 256


def _conv_body(c_ref, ch_ref, gb_ref, gx_ref, gxh_ref, cw_ref, cb_ref, lg_ref, lb_ref, sw_ref,
               ob_ref, oc_ref, cext_ref, y_ref, gext_ref, *, blocks_per_seq):
    first = (pl.program_id(0) % blocks_per_seq) == 0
    cext_ref[0:CONF_HALO, :] = jnp.where(first, 0.0, ch_ref[...])
    cext_ref[CONF_HALO:, :] = c_ref[...]
    gext_ref[0:SC_HALO, :] = jnp.where(first, 0.0, gxh_ref[...])
    gext_ref[SC_HALO:, :] = gx_ref[...]

    for lc in range(CONF_W // CONV_LANES):
        lanes = slice(lc * CONV_LANES, (lc + 1) * CONV_LANES)
        for rb in range(CONV_BT // CONV_ROWS):
            r0 = rb * CONV_ROWS
            acc = jnp.zeros((CONV_ROWS, CONV_LANES), F32)
            for k in range(CONF_K):
                start = r0 + CONF_HALO - (CONF_K - 1) + k
                acc = acc + cw_ref[k:k + 1, lanes] * cext_ref[start:start + CONV_ROWS, lanes]
            y_ref[r0:r0 + CONV_ROWS, lanes] = acc + cb_ref[:, lanes]
            acc2 = jnp.zeros((CONV_ROWS, CONV_LANES), F32)
            for k in range(SC_K):
                start = r0 + SC_HALO - (SC_K - 1) + k
                acc2 = acc2 + sw_ref[k:k + 1, lanes] * gext_ref[start:start + CONV_ROWS, lanes]
            oc_ref[r0:r0 + CONV_ROWS, lanes] = (
                gb_ref[r0:r0 + CONV_ROWS, lanes] * acc2).astype(oc_ref.dtype)

    y = y_ref[...]
    mu = jnp.mean(y, axis=-1, keepdims=True)
    yc = y - mu
    var = jnp.mean(yc * yc, axis=-1, keepdims=True)
    yn = yc * lax.rsqrt(var + LN_EPS) * lg_ref[...] + lb_ref[...]
    ob_ref[...] = (yn * jax.nn.sigmoid(yn)).astype(ob_ref.dtype)


def _conv_mixers(c, gb, gx, conf_dw, conf_dw_b, conf_ln_g, conf_ln_b, sc_dw, seq_len):
    M = c.shape[0]
    bt = CONV_BT
    row = lambda i: (i, 0)
    whole = lambda i: (0, 0)
    in_specs = [
        pl.BlockSpec((bt, CONF_W), row),
        pl.BlockSpec((CONF_HALO, CONF_W), lambda i: (jnp.maximum(i * (bt // CONF_HALO) - 1, 0), 0)),
        pl.BlockSpec((bt, SC_W), row),
        pl.BlockSpec((bt, SC_W), row),
        pl.BlockSpec((SC_HALO, SC_W), lambda i: (jnp.maximum(i * (bt // SC_HALO) - 1, 0), 0)),
        pl.BlockSpec((CONF_K, CONF_W), whole),
        pl.BlockSpec((1, CONF_W), whole),
        pl.BlockSpec((1, CONF_W), whole),
        pl.BlockSpec((1, CONF_W), whole),
        pl.BlockSpec((SC_K, SC_W), whole),
    ]
    est = 2 * (3 * _nbytes((bt, CONF_W), F32) + 2 * _nbytes((bt, CONF_W), BF16)) \
        + 6 * _nbytes((bt + CONF_HALO, CONF_W), F32)
    return pl.pallas_call(
        functools.partial(_conv_body, blocks_per_seq=seq_len // bt),
        grid=(M // bt,),
        in_specs=in_specs,
        out_specs=[pl.BlockSpec((bt, CONF_W), row), pl.BlockSpec((bt, SC_W), row)],
        out_shape=[jax.ShapeDtypeStruct((M, CONF_W), BF16), jax.ShapeDtypeStruct((M, SC_W), BF16)],
        scratch_shapes=[pltpu.VMEM((bt + CONF_HALO, CONF_W), F32),
                        pltpu.VMEM((bt, CONF_W), F32),
                        pltpu.VMEM((bt + SC_HALO, SC_W), F32)],
        compiler_params=_compiler_params(1, est), name="conv_mixers",
    )(c, c, gb, gx, gx, conf_dw, conf_dw_b.reshape(1, -1), conf_ln_g.reshape(1, -1),
      conf_ln_b.reshape(1, -1), sc_dw)


def _merge_body(x_ref, oa_ref, ob_ref, oc_ref, wg0_ref, wg1_ref, wg2_ref,
                pa_ref, pb_ref, pc_ref, out_ref):
    x = x_ref[...]
    acc = None
    for o_ref, wg_ref, p_ref in ((oa_ref, wg0_ref, pa_ref), (ob_ref, wg1_ref, pb_ref),
                                 (oc_ref, wg2_ref, pc_ref)):
        gate = jax.nn.sigmoid(jnp.dot(x, wg_ref[...], preferred_element_type=F32))
        term = gate * jnp.dot(o_ref[...], p_ref[...], preferred_element_type=F32)
        acc = term if acc is None else acc + term
    out_ref[...] = acc.astype(out_ref.dtype)


def _merge(xb, oa, ob, oc, w_in_b, pa, pb, pc, *, bm, bn):
    M = xb.shape[0]
    lhs_spec = lambda a: pl.BlockSpec((bm, a.shape[1]), lambda m, n: (m, 0))
    gate_spec = lambda i: pl.BlockSpec(
        (D_MODEL, bn), functools.partial(_col_index, base=(OFF_GATE + i * D_MODEL) // bn))
    p_spec = lambda p: pl.BlockSpec((p.shape[0], bn), lambda m, n: (0, n))
    k_total = D_MODEL + ATTN_W + CONF_W + SC_W
    est = 2 * (_nbytes((bm, k_total), BF16) + _nbytes((3 * D_MODEL + k_total - D_MODEL, bn), BF16)
               + _nbytes((bm, bn), BF16)) + 8 * _nbytes((bm, bn), F32)
    return pl.pallas_call(
        _merge_body,
        grid=(M // bm, D_MODEL // bn),
        in_specs=[lhs_spec(xb), lhs_spec(oa), lhs_spec(ob), lhs_spec(oc),
                  gate_spec(0), gate_spec(1), gate_spec(2), p_spec(pa), p_spec(pb), p_spec(pc)],
        out_specs=pl.BlockSpec((bm, bn), lambda m, n: (m, n)),
        out_shape=jax.ShapeDtypeStruct((M, D_MODEL), BF16),
        compiler_params=_compiler_params(2, est), name="gated_merge",
    )(xb, oa, ob, oc, w_in_b, w_in_b, w_in_b, pa, pb, pc)


MM_COLS = 512


def _mm_res_ln_body(a_ref, w_ref, res_ref, g_ref, b_ref, o32_ref, obf_ref, *, nk, n_res):
    k = pl.program_id(1)
    width = o32_ref.shape[1]
    res_w = width // n_res

    @pl.when(k == 0)
    def _():
        o32_ref[...] = jnp.zeros_like(o32_ref)

    a = a_ref[...]
    for j in range(width // MM_COLS):
        cols = slice(j * MM_COLS, (j + 1) * MM_COLS)
        o32_ref[:, cols] += jnp.dot(a, w_ref[:, cols], preferred_element_type=F32)

    for j in range(n_res):
        @pl.when(k == j)
        def _(j=j):
            cols = slice(j * res_w, (j + 1) * res_w)
            o32_ref[:, cols] += ALPHA * res_ref[...]

    @pl.when(k == nk - 1)
    def _():
        y = o32_ref[...]
        mu = jnp.mean(y, axis=-1, keepdims=True)
        yc = y - mu
        var = jnp.mean(yc * yc, axis=-1, keepdims=True)
        out = yc * lax.rsqrt(var + LN_EPS) * g_ref[...] + b_ref[...]
        o32_ref[...] = out
        obf_ref[...] = out.astype(obf_ref.dtype)


def _mm_res_ln(a, w, res, g, b, *, bm, bk, name):
    M, K = a.shape
    width = w.shape[1]
    nk = K // bk
    n_res = min(nk, 8)
    res_w = width // n_res
    est = 2 * (_nbytes((bm, bk), BF16) + _nbytes((bk, width), BF16) + _nbytes((bm, res_w), F32)
               + _nbytes((bm, width), F32) + _nbytes((bm, width), BF16)) \
        + 4 * _nbytes((bm, MM_COLS), F32)
    return pl.pallas_call(
        functools.partial(_mm_res_ln_body, nk=nk, n_res=n_res),
        grid=(M // bm, nk),
        in_specs=[pl.BlockSpec((bm, bk), lambda m, k: (m, k)),
                  pl.BlockSpec((bk, width), lambda m, k: (k, 0)),
                  pl.BlockSpec((bm, res_w), lambda m, k: (m, jnp.minimum(k, n_res - 1))),
                  pl.BlockSpec((1, width), lambda m, k: (0, 0)),
                  pl.BlockSpec((1, width), lambda m, k: (0, 0))],
        out_specs=[pl.BlockSpec((bm, width), lambda m, k: (m, 0)),
                   pl.BlockSpec((bm, width), lambda m, k: (m, 0))],
        out_shape=[jax.ShapeDtypeStruct((M, width), F32), jax.ShapeDtypeStruct((M, width), BF16)],
        compiler_params=_compiler_params(2, est), name=name,
    )(a, w, res, g.reshape(1, -1), b.reshape(1, -1))


def _layer(h32, hbf, seq_len, w_in, sinks, conf_dw, conf_dw_b, conf_ln_g, conf_ln_b, sc_dw,
           w_proj_attn, w_proj_conf, w_proj_sc, w_out, ln1_g, ln1_b, w_ffn_in, w_ffn_down,
           ln2_g, ln2_b):
    w_in_b = w_in.astype(BF16)
    (qkv,) = _proj(hbf, w_in_b, (OFF_Q,), _qkv_epilogue, (BF16,),
                   n_cols=QKV_W, bm=1024, bn=512, name="qkv_proj")
    c, gb, gx = _proj(hbf, w_in_b, (OFF_CONF_VAL, OFF_CONF_GATE, OFF_SC_B, OFF_SC_C, OFF_SC_X),
                      _mixer_epilogue, (F32, F32, F32), n_cols=CONF_W, bm=1024, bn=256,
                      name="mixer_proj")
    o_a = _attention(qkv, sinks, seq_len)
    o_b, o_c = _conv_mixers(c, gb, gx, conf_dw, conf_dw_b, conf_ln_g, conf_ln_b, sc_dw, seq_len)
    merged = _merge(hbf, o_a, o_b, o_c, w_in_b, w_proj_attn.astype(BF16),
                    w_proj_conf.astype(BF16), w_proj_sc.astype(BF16), bm=512, bn=256)
    x32, xbf = _mm_res_ln(merged, w_out.astype(BF16), h32, ln1_g, ln1_b,
                          bm=512, bk=512, name="out_proj_ln")

    pad = D_FF_PAD - D_FF
    w_ffn_in_b = jnp.concatenate(
        [jnp.pad(w_ffn_in[:, :D_FF], ((0, 0), (0, pad))),
         jnp.pad(w_ffn_in[:, D_FF:], ((0, 0), (0, pad)))], axis=1).astype(BF16)
    w_ffn_down_b = jnp.pad(w_ffn_down, ((0, pad), (0, 0))).astype(BF16)
    (act,) = _proj(xbf, w_ffn_in_b, (0, D_FF_PAD), _swiglu_epilogue, (BF16,),
                   n_cols=D_FF_PAD, bm=1024, bn=512, name="ffn_in_swiglu")
    return _mm_res_ln(act, w_ffn_down_b, x32, ln2_g, ln2_b, bm=512, bk=1024, name="ffn_down_ln")


def kernel(x, w_in, attn_sinks, conf_dw, conf_dw_b, conf_ln_g, conf_ln_b, sc_dw, w_proj_attn,
           w_proj_conf, w_proj_sc, w_out, ln1_g, ln1_b, w_ffn_in, w_ffn_down, ln2_g, ln2_b):
    batch, seq_len, d_model = x.shape
    h32 = x.reshape(batch * seq_len, d_model)
    hbf = h32.astype(BF16)
    for l in range(DEPTH):
        h32, hbf = _layer(h32, hbf, seq_len, w_in[l], attn_sinks[l], conf_dw[l], conf_dw_b[l],
                          conf_ln_g[l], conf_ln_b[l], sc_dw[l], w_proj_attn[l], w_proj_conf[l],
                          w_proj_sc[l], w_out[l], ln1_g[l], ln1_b[l], w_ffn_in[l],
                          w_ffn_down[l], ln2_g[l], ln2_b[l])
    return h32.reshape(batch, seq_len, d_model)
```

```python
import functools

import jax
import jax.numpy as jnp
from jax import lax
from jax.experimental import pallas as pl
from jax.experimental.pallas import tpu as pltpu

D_MODEL = 4096
DEPTH = 2
HEAD_DIM = 64
N_Q_HEADS = 32
N_KV_HEADS = 4
GROUP = 8
BLOCK = 128
ATTN_W = N_Q_HEADS * HEAD_DIM
KV_W = N_KV_HEADS * HEAD_DIM
CONF_W = 1024
CONF_K = 31
SC_W = 1024
SC_K = 3
D_FF = 11008
ALPHA = (2 * DEPTH) ** 0.25
LN_EPS = 1e-5
NEG_INF = -1e30

OFF_Q = 0
OFF_K = ATTN_W
OFF_V = ATTN_W + KV_W
OFF_CONF_VAL = ATTN_W + 2 * KV_W
OFF_CONF_GATE = OFF_CONF_VAL + CONF_W
OFF_SC_B = OFF_CONF_GATE + CONF_W
OFF_SC_C = OFF_SC_B + SC_W
OFF_SC_X = OFF_SC_C + SC_W
OFF_GATE = OFF_SC_X + SC_W
QKV_W = ATTN_W + 2 * KV_W

V7X_VMEM_BYTES = 64 * 1024 * 1024
V7X_VMEM_CAP = V7X_VMEM_BYTES - 6 * 1024 * 1024
LANE = 128

D_FF_PAD = 11264

F32 = jnp.float32
BF16 = jnp.bfloat16


def _compiler_params(n_grid, vmem_estimate):
    limit = min(V7X_VMEM_CAP, int(vmem_estimate * 1.2) + (8 << 20))
    return pltpu.CompilerParams(
        dimension_semantics=("arbitrary",) * n_grid, vmem_limit_bytes=limit)


def _nbytes(shape, dtype):
    n = 1
    for s in shape:
        n *= s
    return n * jnp.dtype(dtype).itemsize


def _proj_body(*refs, n_dots, epilogue, n_real, n_total):
    x_ref = refs[0]
    w_refs = refs[1:1 + n_dots]
    o_refs = refs[1 + n_dots:]
    def compute():
        x = x_ref[...]
        zs = [jnp.dot(x, w_ref[...], preferred_element_type=F32) for w_ref in w_refs]
        outs = epilogue(*zs)
        for o_ref, o in zip(o_refs, outs):
            o_ref[...] = o.astype(o_ref.dtype)

    if n_real == n_total:
        compute()
        return

    n = pl.program_id(1)
    pl.when(n < n_real)(compute)

    @pl.when(n >= n_real)
    def _():
        for o_ref in o_refs:
            o_ref[...] = jnp.zeros_like(o_ref)


def _col_index(m, n, *, layer, base, last):
    return (layer, 0, base + jnp.minimum(n, last))


def _proj(lhs, w, layer, col_offsets, epilogue, out_dtypes, *, n_cols, out_cols, bm, bn, name):
    M, K = lhs.shape
    n_real = n_cols // bn
    in_specs = [pl.BlockSpec((bm, K), lambda m, n: (m, 0))]
    for off in col_offsets:
        assert off % bn == 0
        in_specs.append(pl.BlockSpec(
            (None, K, bn),
            functools.partial(_col_index, layer=layer, base=off // bn, last=n_real - 1)))
    out_specs = [pl.BlockSpec((bm, bn), lambda m, n: (m, n)) for _ in out_dtypes]
    out_shape = [jax.ShapeDtypeStruct((M, out_cols), dt) for dt in out_dtypes]
    est = 2 * (_nbytes((bm, K), lhs.dtype) + len(col_offsets) * _nbytes((K, bn), w.dtype)
               + sum(_nbytes((bm, bn), dt) for dt in out_dtypes))
    est += (len(col_offsets) + len(out_dtypes)) * _nbytes((bm, bn), F32)
    return pl.pallas_call(
        functools.partial(_proj_body, n_dots=len(col_offsets), epilogue=epilogue, n_real=n_real,
                          n_total=out_cols // bn),
        grid=(M // bm, out_cols // bn),
        in_specs=in_specs, out_specs=out_specs, out_shape=out_shape,
        compiler_params=_compiler_params(2, est), name=name,
    )(lhs, *([w] * len(col_offsets)))


def _qkv_epilogue(z):
    return (z,)


def _mixer_epilogue(c_val, c_gate, g_b, g_c, x_in):
    return (c_val * jax.nn.sigmoid(c_gate), g_b, g_c * x_in)


def _swiglu_epilogue(f_gate, f_up):
    return (jax.nn.silu(f_gate) * f_up,)


def _attn_body(sink_ref, q_ref, kp_ref, kc_ref, vp_ref, vc_ref, bias_ref, o_ref):
    bias = bias_ref[0]
    lo = lax.broadcasted_iota(jnp.int32, (2 * BLOCK, LANE), 1) < HEAD_DIM
    lo_out = lax.broadcasted_iota(jnp.int32, (4 * BLOCK, LANE), 1) < HEAD_DIM
    scale = HEAD_DIM ** -0.5
    for j in range(N_KV_HEADS // 2):
        lanes = slice(j * LANE, (j + 1) * LANE)
        kcat = jnp.concatenate([kp_ref[:, lanes], kc_ref[:, lanes]], axis=0).astype(F32)
        vcat = jnp.concatenate([vp_ref[:, lanes], vc_ref[:, lanes]], axis=0).astype(F32)
        krot = pltpu.roll(kcat, HEAD_DIM, axis=1)
        vrot = pltpu.roll(vcat, HEAD_DIM, axis=1)
        for hh in range(2):
            h = 2 * j + hh
            k_lo, k_hi = (kcat, krot) if hh == 0 else (krot, kcat)
            v_lo, v_hi = (vcat, vrot) if hh == 0 else (vrot, vcat)
            kk = jnp.concatenate([jnp.where(lo, k_lo, 0.0), jnp.where(lo, 0.0, k_hi)],
                                 axis=0).astype(BF16)
            vv = jnp.concatenate([jnp.where(lo, v_lo, 0.0), jnp.where(lo, 0.0, v_hi)],
                                 axis=0).astype(BF16)
            base = h * GROUP * HEAD_DIM
            qs = jnp.concatenate(
                [q_ref[:, base + p * LANE: base + (p + 1) * LANE] for p in range(GROUP // 2)],
                axis=0)
            s = lax.dot_general(qs, kk, (((1,), (1,)), ((), ())),
                                preferred_element_type=F32)
            s = s * scale + bias
            sink_e = [jnp.concatenate(
                [jnp.full((BLOCK, 1), sink_ref[h * GROUP + 2 * p + e], F32)
                 for p in range(GROUP // 2)], axis=0) for e in range(2)]
            ps, dens = [], []
            for e in range(2):
                se = s[:, e * 2 * BLOCK:(e + 1) * 2 * BLOCK]
                m = jnp.maximum(jnp.max(se, axis=-1, keepdims=True), sink_e[e])
                pe = jnp.exp(se - m)
                dens.append(jnp.sum(pe, axis=-1, keepdims=True) + jnp.exp(sink_e[e] - m))
                ps.append(pe.astype(BF16))
            pcat = jnp.concatenate(ps, axis=1)
            o = jnp.dot(pcat, vv, preferred_element_type=F32)
            o = o / jnp.where(lo_out, dens[0], dens[1])
            for p in range(GROUP // 2):
                o_ref[:, base + p * LANE: base + (p + 1) * LANE] = (
                    o[p * BLOCK:(p + 1) * BLOCK].astype(o_ref.dtype))


def _attn_bias():
    r = jnp.arange(4 * BLOCK)[:, None] % BLOCK
    c = jnp.arange(4 * BLOCK)[None, :] % (2 * BLOCK)
    band = (c - r >= 1) & (c - r <= BLOCK)
    first = band & (c >= BLOCK)
    return jnp.where(jnp.stack([band, first]), 0.0, NEG_INF).astype(F32)


def _attention(qkv, sinks, seq_len):
    M = qkv.shape[0]
    blocks_per_seq = seq_len // BLOCK
    k_col = OFF_K // KV_W
    v_col = OFF_V // KV_W

    def prev(i):
        return jnp.maximum(i - 1, 0)

    in_specs = [
        pl.BlockSpec(memory_space=pltpu.SMEM),
        pl.BlockSpec((BLOCK, ATTN_W), lambda i: (i, 0)),
        pl.BlockSpec((BLOCK, KV_W), lambda i: (prev(i), k_col)),
        pl.BlockSpec((BLOCK, KV_W), lambda i: (i, k_col)),
        pl.BlockSpec((BLOCK, KV_W), lambda i: (prev(i), v_col)),
        pl.BlockSpec((BLOCK, KV_W), lambda i: (i, v_col)),
        pl.BlockSpec((1, 4 * BLOCK, 4 * BLOCK),
                     lambda i: (jnp.where(i % blocks_per_seq == 0, 1, 0), 0, 0)),
    ]
    est = 2 * (2 * _nbytes((BLOCK, ATTN_W), BF16) + 4 * _nbytes((BLOCK, KV_W), BF16)
               + _nbytes((4 * BLOCK, 4 * BLOCK), F32)) + 16 * _nbytes((4 * BLOCK, 4 * BLOCK), F32)
    return pl.pallas_call(
        _attn_body,
        grid=(M // BLOCK,),
        in_specs=in_specs,
        out_specs=pl.BlockSpec((BLOCK, ATTN_W), lambda i: (i, 0)),
        out_shape=jax.ShapeDtypeStruct((M, ATTN_W), BF16),
        compiler_params=_compiler_params(1, est), name="swa_attention",
    )(sinks, qkv, qkv, qkv, qkv, qkv, _attn_bias())


CONV_BT = 256
CONF_HALO = 32
SC_HALO = 8
CONV_ROWS = 64
CONV_LANES = 256


def _conv_body(c_ref, ch_ref, gb_ref, gx_ref, gxh_ref, cw_ref, cb_ref, lg_ref, lb_ref, sw_ref,
               ob_ref, oc_ref, cext_ref, y_ref, gext_ref, *, blocks_per_seq):
    first = (pl.program_id(0) % blocks_per_seq) == 0
    cext_ref[0:CONF_HALO, :] = jnp.where(first, 0.0, ch_ref[...])
    cext_ref[CONF_HALO:, :] = c_ref[...]
    gext_ref[0:SC_HALO, :] = jnp.where(first, 0.0, gxh_ref[...])
    gext_ref[SC_HALO:, :] = gx_ref[...]

    for lc in range(CONF_W // CONV_LANES):
        lanes = slice(lc * CONV_LANES, (lc + 1) * CONV_LANES)
        for rb in range(CONV_BT // CONV_ROWS):
            r0 = rb * CONV_ROWS
            acc = jnp.zeros((CONV_ROWS, CONV_LANES), F32)
            for k in range(CONF_K):
                start = r0 + CONF_HALO - (CONF_K - 1) + k
                acc = acc + cw_ref[k:k + 1, lanes] * cext_ref[start:start + CONV_ROWS, lanes]
            y_ref[r0:r0 + CONV_ROWS, lanes] = acc + cb_ref[:, lanes]
            acc2 = jnp.zeros((CONV_ROWS, CONV_LANES), F32)
            for k in range(SC_K):
                start = r0 + SC_HALO - (SC_K - 1) + k
                acc2 = acc2 + sw_ref[k:k + 1, lanes] * gext_ref[start:start + CONV_ROWS, lanes]
            oc_ref[r0:r0 + CONV_ROWS, lanes] = (
                gb_ref[r0:r0 + CONV_ROWS, lanes] * acc2).astype(oc_ref.dtype)

    y = y_ref[...]
    mu = jnp.mean(y, axis=-1, keepdims=True)
    yc = y - mu
    var = jnp.mean(yc * yc, axis=-1, keepdims=True)
    yn = yc * lax.rsqrt(var + LN_EPS) * lg_ref[...] + lb_ref[...]
    ob_ref[...] = (yn * jax.nn.sigmoid(yn)).astype(ob_ref.dtype)


def _conv_mixers(c, gb, gx, conf_dw, conf_dw_b, conf_ln_g, conf_ln_b, sc_dw, seq_len):
    M = c.shape[0]
    bt = CONV_BT
    row = lambda i: (i, 0)
    whole = lambda i: (0, 0)
    in_specs = [
        pl.BlockSpec((bt, CONF_W), row),
        pl.BlockSpec((CONF_HALO, CONF_W), lambda i: (jnp.maximum(i * (bt // CONF_HALO) - 1, 0), 0)),
        pl.BlockSpec((bt, SC_W), row),
        pl.BlockSpec((bt, SC_W), row),
        pl.BlockSpec((SC_HALO, SC_W), lambda i: (jnp.maximum(i * (bt // SC_HALO) - 1, 0), 0)),
        pl.BlockSpec((CONF_K, CONF_W), whole),
        pl.BlockSpec((1, CONF_W), whole),
        pl.BlockSpec((1, CONF_W), whole),
        pl.BlockSpec((1, CONF_W), whole),
        pl.BlockSpec((SC_K, SC_W), whole),
    ]
    est = 2 * (3 * _nbytes((bt, CONF_W), F32) + 2 * _nbytes((bt, CONF_W), BF16)) \
        + 6 * _nbytes((bt + CONF_HALO, CONF_W), F32)
    return pl.pallas_call(
        functools.partial(_conv_body, blocks_per_seq=seq_len // bt),
        grid=(M // bt,),
        in_specs=in_specs,
        out_specs=[pl.BlockSpec((bt, CONF_W), row), pl.BlockSpec((bt, SC_W), row)],
        out_shape=[jax.ShapeDtypeStruct((M, CONF_W), BF16), jax.ShapeDtypeStruct((M, SC_W), BF16)],
        scratch_shapes=[pltpu.VMEM((bt + CONF_HALO, CONF_W), F32),
                        pltpu.VMEM((bt, CONF_W), F32),
                        pltpu.VMEM((bt + SC_HALO, SC_W), F32)],
        compiler_params=_compiler_params(1, est), name="conv_mixers",
    )(c, c, gb, gx, gx, conf_dw, conf_dw_b.reshape(1, -1), conf_ln_g.reshape(1, -1),
      conf_ln_b.reshape(1, -1), sc_dw)


MERGE_ROWS = 512


def _merge_body(x_ref, oa_ref, ob_ref, oc_ref, wg0_ref, wg1_ref, wg2_ref,
                pa_ref, pb_ref, pc_ref, out_ref):
    for r in range(out_ref.shape[0] // MERGE_ROWS):
        rows = slice(r * MERGE_ROWS, (r + 1) * MERGE_ROWS)
        acc = None
        for o_ref, wg_ref, p_ref in ((oa_ref, wg0_ref, pa_ref), (ob_ref, wg1_ref, pb_ref),
                                     (oc_ref, wg2_ref, pc_ref)):
            gate = jax.nn.sigmoid(
                jnp.dot(x_ref[rows, :], wg_ref[...], preferred_element_type=F32))
            term = gate * jnp.dot(o_ref[rows, :], p_ref[...], preferred_element_type=F32)
            acc = term if acc is None else acc + term
        out_ref[rows, :] = acc.astype(out_ref.dtype)


def _merge(xb, oa, ob, oc, w_in_b, pa, pb, pc, layer, *, bm, bn):
    M = xb.shape[0]
    n_blocks = D_MODEL // bn
    lhs_spec = lambda a: pl.BlockSpec((bm, a.shape[1]), lambda m, n: (m, 0))
    gate_spec = lambda i: pl.BlockSpec(
        (None, D_MODEL, bn),
        functools.partial(_col_index, layer=layer, base=(OFF_GATE + i * D_MODEL) // bn,
                          last=n_blocks - 1))
    p_spec = lambda p: pl.BlockSpec((None, p.shape[1], bn), lambda m, n: (layer, 0, n))
    k_total = D_MODEL + ATTN_W + CONF_W + SC_W
    est = 2 * (_nbytes((bm, k_total), BF16) + _nbytes((3 * D_MODEL + k_total - D_MODEL, bn), BF16)
               + _nbytes((bm, bn), BF16)) + 8 * _nbytes((bm, bn), F32)
    return pl.pallas_call(
        _merge_body,
        grid=(M // bm, D_MODEL // bn),
        in_specs=[lhs_spec(xb), lhs_spec(oa), lhs_spec(ob), lhs_spec(oc),
                  gate_spec(0), gate_spec(1), gate_spec(2), p_spec(pa), p_spec(pb), p_spec(pc)],
        out_specs=pl.BlockSpec((bm, bn), lambda m, n: (m, n)),
        out_shape=jax.ShapeDtypeStruct((M, D_MODEL), BF16),
        compiler_params=_compiler_params(2, est), name="gated_merge",
    )(xb, oa, ob, oc, w_in_b, w_in_b, w_in_b, pa, pb, pc)


MM_COLS = 512


LN_ROWS = 64


def _layer_norm_rows(y_ref, g_ref, b_ref, o32_ref, obf_ref):
    def chunk(i, carry):
        rows = pl.ds(pl.multiple_of(i * LN_ROWS, LN_ROWS), LN_ROWS)
        y = y_ref[rows, :]
        mu = jnp.mean(y, axis=-1, keepdims=True)
        yc = y - mu
        var = jnp.mean(yc * yc, axis=-1, keepdims=True)
        out = yc * lax.rsqrt(var + LN_EPS) * g_ref[...] + b_ref[...]
        o32_ref[rows, :] = out
        obf_ref[rows, :] = out.astype(obf_ref.dtype)
        return carry

    lax.fori_loop(0, y_ref.shape[0] // LN_ROWS, chunk, 0)


def _mm_acc_res_ln_body(a_ref, w_ref, res_ref, g_ref, b_ref, o32_ref, obf_ref, *, nk, n_res):
    k = pl.program_id(1)
    width = o32_ref.shape[1]
    res_w = width // n_res

    @pl.when(k == 0)
    def _():
        o32_ref[...] = jnp.zeros_like(o32_ref)

    for j in range(width // MM_COLS):
        cols = slice(j * MM_COLS, (j + 1) * MM_COLS)
        o32_ref[:, cols] += jnp.dot(a_ref[...], w_ref[:, cols], preferred_element_type=F32)

    for j in range(n_res):
        @pl.when(k == j)
        def _(j=j):
            cols = slice(j * res_w, (j + 1) * res_w)
            o32_ref[:, cols] += ALPHA * res_ref[...]

    @pl.when(k == nk - 1)
    def _():
        _layer_norm_rows(o32_ref, g_ref, b_ref, o32_ref, obf_ref)


def _mm_acc_res_ln(a, w, layer, res, g, b, *, bm, bk, name):
    M, K = a.shape
    width = w.shape[2]
    nk = K // bk
    n_res = min(nk, 8)
    res_w = width // n_res
    est = 2 * (_nbytes((bm, bk), BF16) + _nbytes((bk, width), BF16) + _nbytes((bm, res_w), F32)
               + _nbytes((bm, width), F32) + _nbytes((bm, width), BF16)) \
        + 4 * _nbytes((bm, MM_COLS), F32)
    return pl.pallas_call(
        functools.partial(_mm_acc_res_ln_body, nk=nk, n_res=n_res),
        grid=(M // bm, nk),
        in_specs=[pl.BlockSpec((bm, bk), lambda m, k: (m, k)),
                  pl.BlockSpec((None, bk, width), lambda m, k: (layer, k, 0)),
                  pl.BlockSpec((bm, res_w), lambda m, k: (m, jnp.minimum(k, n_res - 1))),
                  pl.BlockSpec((1, width), lambda m, k: (0, 0)),
                  pl.BlockSpec((1, width), lambda m, k: (0, 0))],
        out_specs=[pl.BlockSpec((bm, width), lambda m, k: (m, 0)),
                   pl.BlockSpec((bm, width), lambda m, k: (m, 0))],
        out_shape=[jax.ShapeDtypeStruct((M, width), F32), jax.ShapeDtypeStruct((M, width), BF16)],
        compiler_params=_compiler_params(2, est), name=name,
    )(a, w, res, g.reshape(1, -1), b.reshape(1, -1))


def _mm_cols_res_ln_body(a_ref, w_ref, res_ref, g_ref, b_ref, o32_ref, obf_ref, *, n_steps):
    n = pl.program_id(1)
    bn = w_ref.shape[1]

    for j in range(n_steps):
        @pl.when(n == j)
        def _(j=j):
            for c in range(bn // MM_COLS):
                src = slice(c * MM_COLS, (c + 1) * MM_COLS)
                dst = slice(j * bn + c * MM_COLS, j * bn + (c + 1) * MM_COLS)
                o32_ref[:, dst] = ALPHA * res_ref[:, src] + jnp.dot(
                    a_ref[...], w_ref[:, src], preferred_element_type=F32)

    @pl.when(n == n_steps - 1)
    def _():
        _layer_norm_rows(o32_ref, g_ref, b_ref, o32_ref, obf_ref)


def _mm_cols_res_ln(a, w, layer, res, g, b, *, bm, bn, name):
    M, K = a.shape
    width = w.shape[2]
    n_steps = width // bn
    est = 2 * (_nbytes((bm, K), BF16) + _nbytes((K, bn), BF16) + _nbytes((bm, bn), F32)
               + _nbytes((bm, width), F32) + _nbytes((bm, width), BF16)) \
        + 2 * _nbytes((bm, MM_COLS), F32)
    return pl.pallas_call(
        functools.partial(_mm_cols_res_ln_body, n_steps=n_steps),
        grid=(M // bm, n_steps),
        in_specs=[pl.BlockSpec((bm, K), lambda m, n: (m, 0)),
                  pl.BlockSpec((None, K, bn), lambda m, n: (layer, 0, n)),
                  pl.BlockSpec((bm, bn), lambda m, n: (m, n)),
                  pl.BlockSpec((1, width), lambda m, n: (0, 0)),
                  pl.BlockSpec((1, width), lambda m, n: (0, 0))],
        out_specs=[pl.BlockSpec((bm, width), lambda m, n: (m, 0)),
                   pl.BlockSpec((bm, width), lambda m, n: (m, 0))],
        out_shape=[jax.ShapeDtypeStruct((M, width), F32), jax.ShapeDtypeStruct((M, width), BF16)],
        compiler_params=_compiler_params(2, est), name=name,
    )(a, w, res, g.reshape(1, -1), b.reshape(1, -1))


def _layer(l, h32, hbf, seq_len, wts, sinks, conf_dw, conf_dw_b, conf_ln_g, conf_ln_b, sc_dw,
           ln1_g, ln1_b, ln2_g, ln2_b):
    w_in_b, pa_b, pb_b, pc_b, w_out_b, w_ffn_in_b, w_ffn_down_b = wts
    (qkv,) = _proj(hbf, w_in_b, l, (OFF_Q,), _qkv_epilogue, (BF16,),
                   n_cols=QKV_W, out_cols=QKV_W, bm=1024, bn=512, name="qkv_proj")
    c, gb, gx = _proj(hbf, w_in_b, l,
                      (OFF_CONF_VAL, OFF_CONF_GATE, OFF_SC_B, OFF_SC_C, OFF_SC_X),
                      _mixer_epilogue, (F32, F32, F32), n_cols=CONF_W, out_cols=CONF_W,
                      bm=1024, bn=256, name="mixer_proj")
    o_a = _attention(qkv, sinks, seq_len)
    o_b, o_c = _conv_mixers(c, gb, gx, conf_dw, conf_dw_b, conf_ln_g, conf_ln_b, sc_dw, seq_len)
    merged = _merge(hbf, o_a, o_b, o_c, w_in_b, pa_b, pb_b, pc_b, l, bm=1024, bn=256)
    x32, xbf = _mm_cols_res_ln(merged, w_out_b, l, h32, ln1_g, ln1_b,
                               bm=512, bn=1024, name="out_proj_ln")
    (act,) = _proj(xbf, w_ffn_in_b, l, (0, D_FF), _swiglu_epilogue, (BF16,),
                   n_cols=D_FF, out_cols=D_FF_PAD, bm=2048, bn=256, name="ffn_in_swiglu")
    return _mm_acc_res_ln(act, w_ffn_down_b, l, x32, ln2_g, ln2_b,
                          bm=512, bk=1024, name="ffn_down_ln")


def kernel(x, w_in, attn_sinks, conf_dw, conf_dw_b, conf_ln_g, conf_ln_b, sc_dw, w_proj_attn,
           w_proj_conf, w_proj_sc, w_out, ln1_g, ln1_b, w_ffn_in, w_ffn_down, ln2_g, ln2_b):
    batch, seq_len, d_model = x.shape
    h32 = x.reshape(batch * seq_len, d_model)
    hbf = h32.astype(BF16)
    wts = (w_in.astype(BF16), w_proj_attn.astype(BF16), w_proj_conf.astype(BF16),
           w_proj_sc.astype(BF16), w_out.astype(BF16), w_ffn_in.astype(BF16),
           jnp.pad(w_ffn_down, ((0, 0), (0, D_FF_PAD - D_FF), (0, 0))).astype(BF16))
    for l in range(DEPTH):
        h32, hbf = _layer(l, h32, hbf, seq_len, wts, attn_sinks[l], conf_dw[l], conf_dw_b[l],
                          conf_ln_g[l], conf_ln_b[l], sc_dw[l], ln1_g[l], ln1_b[l],
                          ln2_g[l], ln2_b[l])
    return h32.reshape(batch, seq_len, d_model)
```

```python
import functools

import jax
import jax.numpy as jnp
from jax import lax
from jax.experimental import pallas as pl
from jax.experimental.pallas import tpu as pltpu

D_MODEL = 4096
DEPTH = 2
HEAD_DIM = 64
N_Q_HEADS = 32
N_KV_HEADS = 4
GROUP = 8
BLOCK = 128
ATTN_W = N_Q_HEADS * HEAD_DIM
KV_W = N_KV_HEADS * HEAD_DIM
CONF_W = 1024
CONF_K = 31
SC_W = 1024
SC_K = 3
D_FF = 11008
ALPHA = (2 * DEPTH) ** 0.25
LN_EPS = 1e-5
NEG_INF = -1e30

OFF_Q = 0
OFF_K = ATTN_W
OFF_V = ATTN_W + KV_W
OFF_CONF_VAL = ATTN_W + 2 * KV_W
OFF_CONF_GATE = OFF_CONF_VAL + CONF_W
OFF_SC_B = OFF_CONF_GATE + CONF_W
OFF_SC_C = OFF_SC_B + SC_W
OFF_SC_X = OFF_SC_C + SC_W
OFF_GATE = OFF_SC_X + SC_W
QKV_W = ATTN_W + 2 * KV_W

V7X_VMEM_BYTES = 64 * 1024 * 1024
V7X_VMEM_CAP = V7X_VMEM_BYTES - 6 * 1024 * 1024
LANE = 128
SUBLANE = 8

D_FF_PAD = 11264

F32 = jnp.float32
BF16 = jnp.bfloat16


def _compiler_params(n_grid, vmem_estimate):
    limit = min(V7X_VMEM_CAP, int(vmem_estimate * 1.2) + (8 << 20))
    return pltpu.CompilerParams(
        dimension_semantics=("arbitrary",) * n_grid, vmem_limit_bytes=limit)


def _nbytes(shape, dtype):
    n = 1
    for s in shape:
        n *= s
    return n * jnp.dtype(dtype).itemsize


def _proj_body(*refs, n_dots, epilogue, n_real, n_total):
    x_ref = refs[0]
    w_refs = refs[1:1 + n_dots]
    o_refs = refs[1 + n_dots:]
    def compute():
        x = x_ref[...]
        zs = [jnp.dot(x, w_ref[...], preferred_element_type=F32) for w_ref in w_refs]
        outs = epilogue(*zs)
        for o_ref, o in zip(o_refs, outs):
            o_ref[...] = o.astype(o_ref.dtype)

    if n_real == n_total:
        compute()
        return

    n = pl.program_id(1)
    pl.when(n < n_real)(compute)

    @pl.when(n >= n_real)
    def _():
        for o_ref in o_refs:
            o_ref[...] = jnp.zeros_like(o_ref)


def _col_index(m, n, *, layer, base, last):
    return (layer, 0, base + jnp.minimum(n, last))


def _proj(lhs, w, layer, col_offsets, epilogue, out_dtypes, *, n_cols, out_cols, bm, bn, name):
    M, K = lhs.shape
    n_real = n_cols // bn
    in_specs = [pl.BlockSpec((bm, K), lambda m, n: (m, 0))]
    for off in col_offsets:
        assert off % bn == 0
        in_specs.append(pl.BlockSpec(
            (None, K, bn),
            functools.partial(_col_index, layer=layer, base=off // bn, last=n_real - 1)))
    out_specs = [pl.BlockSpec((bm, bn), lambda m, n: (m, n)) for _ in out_dtypes]
    out_shape = [jax.ShapeDtypeStruct((M, out_cols), dt) for dt in out_dtypes]
    est = 2 * (_nbytes((bm, K), lhs.dtype) + len(col_offsets) * _nbytes((K, bn), w.dtype)
               + sum(_nbytes((bm, bn), dt) for dt in out_dtypes))
    est += (len(col_offsets) + len(out_dtypes)) * _nbytes((bm, bn), F32)
    return pl.pallas_call(
        functools.partial(_proj_body, n_dots=len(col_offsets), epilogue=epilogue, n_real=n_real,
                          n_total=out_cols // bn),
        grid=(M // bm, out_cols // bn),
        in_specs=in_specs, out_specs=out_specs, out_shape=out_shape,
        compiler_params=_compiler_params(2, est), name=name,
    )(lhs, *([w] * len(col_offsets)))


def _qkv_epilogue(z):
    return (z,)


def _mixer_epilogue(c_val, c_gate, g_b, g_c, x_in):
    return (c_val * jax.nn.sigmoid(c_gate), g_b, g_c * x_in)


def _swiglu_epilogue(f_gate, f_up):
    return (jax.nn.silu(f_gate) * f_up,)


def _attn_body(sink_ref, q_ref, kp_ref, kc_ref, vp_ref, vc_ref, bias_ref, o_ref):
    bias = bias_ref[0]
    lo = lax.broadcasted_iota(jnp.int32, (2 * BLOCK, LANE), 1) < HEAD_DIM
    lo_out = lax.broadcasted_iota(jnp.int32, (4 * BLOCK, LANE), 1) < HEAD_DIM
    first_keys = lax.broadcasted_iota(jnp.int32, (4 * BLOCK, LANE), 0) < 2 * BLOCK
    ones_blk = jnp.where(first_keys == lo_out, 1.0, 0.0).astype(BF16)
    scale = HEAD_DIM ** -0.5
    for j in range(N_KV_HEADS // 2):
        lanes = slice(j * LANE, (j + 1) * LANE)
        kcat = jnp.concatenate([kp_ref[:, lanes], kc_ref[:, lanes]], axis=0).astype(F32) * scale
        vcat = jnp.concatenate([vp_ref[:, lanes], vc_ref[:, lanes]], axis=0).astype(F32)
        krot = pltpu.roll(kcat, HEAD_DIM, axis=1)
        vrot = pltpu.roll(vcat, HEAD_DIM, axis=1)
        for hh in range(2):
            h = 2 * j + hh
            k_lo, k_hi = (kcat, krot) if hh == 0 else (krot, kcat)
            v_lo, v_hi = (vcat, vrot) if hh == 0 else (vrot, vcat)
            kk = jnp.concatenate([jnp.where(lo, k_lo, 0.0), jnp.where(lo, 0.0, k_hi)],
                                 axis=0).astype(BF16)
            vv = jnp.concatenate([jnp.where(lo, v_lo, 0.0), jnp.where(lo, 0.0, v_hi)],
                                 axis=0).astype(BF16)
            rhs = jnp.concatenate([vv, ones_blk], axis=1)
            base = h * GROUP * HEAD_DIM
            qs = jnp.concatenate(
                [q_ref[:, base + p * LANE: base + (p + 1) * LANE] for p in range(GROUP // 2)],
                axis=0)
            s = lax.dot_general(qs, kk, (((1,), (1,)), ((), ())),
                                preferred_element_type=F32) + bias
            sink_e = [jnp.concatenate(
                [jnp.full((BLOCK, LANE), sink_ref[h * GROUP + 2 * p + e], F32)
                 for p in range(GROUP // 2)], axis=0) for e in range(2)]
            ps, ms = [], []
            for e in range(2):
                se = s[:, e * 2 * BLOCK:(e + 1) * 2 * BLOCK]
                row_max = jnp.max(se, axis=-1, keepdims=True)
                m = jnp.maximum(jnp.broadcast_to(row_max, (4 * BLOCK, LANE)), sink_e[e])
                ms.append(m)
                ps.append(jnp.exp(se - jnp.concatenate([m, m], axis=1)).astype(BF16))
            pcat = jnp.concatenate(ps, axis=1)
            ov = jnp.dot(pcat, rhs, preferred_element_type=F32)
            m_sel = jnp.where(lo_out, ms[0], ms[1])
            sink_sel = jnp.where(lo_out, sink_e[0], sink_e[1])
            den = ov[:, LANE:] + jnp.exp(sink_sel - m_sel)
            o = ov[:, :LANE] / den
            for p in range(GROUP // 2):
                o_ref[:, base + p * LANE: base + (p + 1) * LANE] = (
                    o[p * BLOCK:(p + 1) * BLOCK].astype(o_ref.dtype))


def _attn_bias():
    r = jnp.arange(4 * BLOCK)[:, None] % BLOCK
    c = jnp.arange(4 * BLOCK)[None, :] % (2 * BLOCK)
    band = (c - r >= 1) & (c - r <= BLOCK)
    first = band & (c >= BLOCK)
    return jnp.where(jnp.stack([band, first]), 0.0, NEG_INF).astype(F32)


def _attention(qkv, sinks, seq_len):
    M = qkv.shape[0]
    blocks_per_seq = seq_len // BLOCK
    k_col = OFF_K // KV_W
    v_col = OFF_V // KV_W

    def prev(i):
        return jnp.maximum(i - 1, 0)

    in_specs = [
        pl.BlockSpec(memory_space=pltpu.SMEM),
        pl.BlockSpec((BLOCK, ATTN_W), lambda i: (i, 0)),
        pl.BlockSpec((BLOCK, KV_W), lambda i: (prev(i), k_col)),
        pl.BlockSpec((BLOCK, KV_W), lambda i: (i, k_col)),
        pl.BlockSpec((BLOCK, KV_W), lambda i: (prev(i), v_col)),
        pl.BlockSpec((BLOCK, KV_W), lambda i: (i, v_col)),
        pl.BlockSpec((1, 4 * BLOCK, 4 * BLOCK),
                     lambda i: (jnp.where(i % blocks_per_seq == 0, 1, 0), 0, 0)),
    ]
    est = 2 * (2 * _nbytes((BLOCK, ATTN_W), BF16) + 4 * _nbytes((BLOCK, KV_W), BF16)
               + _nbytes((4 * BLOCK, 4 * BLOCK), F32)) + 16 * _nbytes((4 * BLOCK, 4 * BLOCK), F32)
    return pl.pallas_call(
        _attn_body,
        grid=(M // BLOCK,),
        in_specs=in_specs,
        out_specs=pl.BlockSpec((BLOCK, ATTN_W), lambda i: (i, 0)),
        out_shape=jax.ShapeDtypeStruct((M, ATTN_W), BF16),
        compiler_params=_compiler_params(1, est), name="swa_attention",
    )(sinks, qkv, qkv, qkv, qkv, qkv, _attn_bias())


CONV_BT = 256
CONF_HALO = 32
SC_HALO = 8
CONV_ROWS = 64
CONV_LANES = 256
CONV_SHIFT_ROWS = CONV_BT + CONF_HALO - SUBLANE


def _conv_body(c_ref, ch_ref, gb_ref, gx_ref, gxh_ref, cw_ref, cb_ref, lg_ref, lb_ref, sw_ref,
               ob_ref, oc_ref, cext_ref, y_ref, gext_ref, shift_ref, *, blocks_per_seq):
    first = (pl.program_id(0) % blocks_per_seq) == 0
    cext_ref[0:CONF_HALO, :] = jnp.where(first, 0.0, ch_ref[...])
    cext_ref[CONF_HALO:, :] = c_ref[...]
    gext_ref[0:SC_HALO, :] = jnp.where(first, 0.0, gxh_ref[...])
    gext_ref[SC_HALO:, :] = gx_ref[...]
    for s in range(1, SUBLANE):
        shift_ref[s - 1] = cext_ref[s:s + CONV_SHIFT_ROWS, :]

    for lc in range(CONF_W // CONV_LANES):
        lanes = slice(lc * CONV_LANES, (lc + 1) * CONV_LANES)
        for rb in range(CONV_BT // CONV_ROWS):
            r0 = rb * CONV_ROWS
            acc = jnp.zeros((CONV_ROWS, CONV_LANES), F32)
            for k in range(CONF_K):
                off = CONF_HALO - (CONF_K - 1) + k
                s, start = off % SUBLANE, r0 + off - off % SUBLANE
                if s == 0:
                    tap = cext_ref[start:start + CONV_ROWS, lanes]
                else:
                    tap = shift_ref[s - 1, start:start + CONV_ROWS, lanes]
                acc = acc + cw_ref[k:k + 1, lanes] * tap
            y_ref[r0:r0 + CONV_ROWS, lanes] = acc + cb_ref[:, lanes]
            acc2 = jnp.zeros((CONV_ROWS, CONV_LANES), F32)
            for k in range(SC_K):
                start = r0 + SC_HALO - (SC_K - 1) + k
                acc2 = acc2 + sw_ref[k:k + 1, lanes] * gext_ref[start:start + CONV_ROWS, lanes]
            oc_ref[r0:r0 + CONV_ROWS, lanes] = (
                gb_ref[r0:r0 + CONV_ROWS, lanes] * acc2).astype(oc_ref.dtype)

    y = y_ref[...]
    mu = jnp.mean(y, axis=-1, keepdims=True)
    yc = y - mu
    var = jnp.mean(yc * yc, axis=-1, keepdims=True)
    yn = yc * lax.rsqrt(var + LN_EPS) * lg_ref[...] + lb_ref[...]
    ob_ref[...] = (yn * jax.nn.sigmoid(yn)).astype(ob_ref.dtype)


def _conv_mixers(c, gb, gx, conf_dw, conf_dw_b, conf_ln_g, conf_ln_b, sc_dw, seq_len):
    M = c.shape[0]
    bt = CONV_BT
    row = lambda i: (i, 0)
    whole = lambda i: (0, 0)
    in_specs = [
        pl.BlockSpec((bt, CONF_W), row),
        pl.BlockSpec((CONF_HALO, CONF_W), lambda i: (jnp.maximum(i * (bt // CONF_HALO) - 1, 0), 0)),
        pl.BlockSpec((bt, SC_W), row),
        pl.BlockSpec((bt, SC_W), row),
        pl.BlockSpec((SC_HALO, SC_W), lambda i: (jnp.maximum(i * (bt // SC_HALO) - 1, 0), 0)),
        pl.BlockSpec((CONF_K, CONF_W), whole),
        pl.BlockSpec((1, CONF_W), whole),
        pl.BlockSpec((1, CONF_W), whole),
        pl.BlockSpec((1, CONF_W), whole),
        pl.BlockSpec((SC_K, SC_W), whole),
    ]
    est = 2 * (3 * _nbytes((bt, CONF_W), F32) + 2 * _nbytes((bt, CONF_W), BF16)) \
        + 6 * _nbytes((bt + CONF_HALO, CONF_W), F32) \
        + _nbytes((SUBLANE - 1, CONV_SHIFT_ROWS, CONF_W), F32)
    return pl.pallas_call(
        functools.partial(_conv_body, blocks_per_seq=seq_len // bt),
        grid=(M // bt,),
        in_specs=in_specs,
        out_specs=[pl.BlockSpec((bt, CONF_W), row), pl.BlockSpec((bt, SC_W), row)],
        out_shape=[jax.ShapeDtypeStruct((M, CONF_W), BF16), jax.ShapeDtypeStruct((M, SC_W), BF16)],
        scratch_shapes=[pltpu.VMEM((bt + CONF_HALO, CONF_W), F32),
                        pltpu.VMEM((bt, CONF_W), F32),
                        pltpu.VMEM((bt + SC_HALO, SC_W), F32),
                        pltpu.VMEM((SUBLANE - 1, CONV_SHIFT_ROWS, CONF_W), F32)],
        compiler_params=_compiler_params(1, est), name="conv_mixers",
    )(c, c, gb, gx, gx, conf_dw, conf_dw_b.reshape(1, -1), conf_ln_g.reshape(1, -1),
      conf_ln_b.reshape(1, -1), sc_dw)


MERGE_ROWS = 512


def _merge_body(x_ref, oa_ref, ob_ref, oc_ref, wg0_ref, wg1_ref, wg2_ref,
                pa_ref, pb_ref, pc_ref, out_ref):
    for r in range(out_ref.shape[0] // MERGE_ROWS):
        rows = slice(r * MERGE_ROWS, (r + 1) * MERGE_ROWS)
        acc = None
        for o_ref, wg_ref, p_ref in ((oa_ref, wg0_ref, pa_ref), (ob_ref, wg1_ref, pb_ref),
                                     (oc_ref, wg2_ref, pc_ref)):
            gate = jax.nn.sigmoid(
                jnp.dot(x_ref[rows, :], wg_ref[...], preferred_element_type=F32))
            term = gate * jnp.dot(o_ref[rows, :], p_ref[...], preferred_element_type=F32)
            acc = term if acc is None else acc + term
        out_ref[rows, :] = acc.astype(out_ref.dtype)


def _merge(xb, oa, ob, oc, w_in_b, pa, pb, pc, layer, *, bm, bn):
    M = xb.shape[0]
    n_blocks = D_MODEL // bn
    lhs_spec = lambda a: pl.BlockSpec((bm, a.shape[1]), lambda m, n: (m, 0))
    gate_spec = lambda i: pl.BlockSpec(
        (None, D_MODEL, bn),
        functools.partial(_col_index, layer=layer, base=(OFF_GATE + i * D_MODEL) // bn,
                          last=n_blocks - 1))
    p_spec = lambda p: pl.BlockSpec((None, p.shape[1], bn), lambda m, n: (layer, 0, n))
    k_total = D_MODEL + ATTN_W + CONF_W + SC_W
    est = 2 * (_nbytes((bm, k_total), BF16) + _nbytes((3 * D_MODEL + k_total - D_MODEL, bn), BF16)
               + _nbytes((bm, bn), BF16)) + 8 * _nbytes((bm, bn), F32)
    return pl.pallas_call(
        _merge_body,
        grid=(M // bm, D_MODEL // bn),
        in_specs=[lhs_spec(xb), lhs_spec(oa), lhs_spec(ob), lhs_spec(oc),
                  gate_spec(0), gate_spec(1), gate_spec(2), p_spec(pa), p_spec(pb), p_spec(pc)],
        out_specs=pl.BlockSpec((bm, bn), lambda m, n: (m, n)),
        out_shape=jax.ShapeDtypeStruct((M, D_MODEL), BF16),
        compiler_params=_compiler_params(2, est), name="gated_merge",
    )(xb, oa, ob, oc, w_in_b, w_in_b, w_in_b, pa, pb, pc)


MM_COLS = 512


MM_ROWS = 512
LN_ROWS = 128
LN_COLS = MM_COLS


def _layer_norm_step(load_piece, n_pieces, g_ref, b_ref, o32_ref, obf_ref):
    width = n_pieces * LN_COLS
    total = None
    for j in range(n_pieces):
        s = jnp.sum(load_piece(j), axis=-1, keepdims=True)
        total = s if total is None else total + s
    mu = total / width
    total = None
    for j in range(n_pieces):
        d = load_piece(j) - mu
        s = jnp.sum(d * d, axis=-1, keepdims=True)
        total = s if total is None else total + s
    rstd = lax.rsqrt(total / width + LN_EPS)
    for j in range(n_pieces):
        cols = slice(j * LN_COLS, (j + 1) * LN_COLS)
        out = (load_piece(j) - mu) * rstd * g_ref[:, cols] + b_ref[:, cols]
        o32_ref[:, cols] = out
        obf_ref[:, cols] = out.astype(obf_ref.dtype)


def _ln_rows(step):
    return pl.ds(pl.multiple_of(step * LN_ROWS, LN_ROWS), LN_ROWS)


def _ln_out_specs(n_main, ln_steps, width):
    index = lambda m, s: (m * ln_steps + jnp.maximum(s - n_main, 0), 0)
    return [pl.BlockSpec((LN_ROWS, width), index), pl.BlockSpec((LN_ROWS, width), index)]


def _mm_acc_res_ln_body(a_ref, w_ref, res_ref, g_ref, b_ref, o32_ref, obf_ref, acc_ref,
                        *, nk, n_res):
    k = pl.program_id(1)
    bm, width = acc_ref.shape
    res_w = width // n_res

    @pl.when(k == 0)
    def _():
        acc_ref[...] = jnp.zeros_like(acc_ref)

    @pl.when(k < nk)
    def _():
        for r in range(bm // MM_ROWS):
            rows = slice(r * MM_ROWS, (r + 1) * MM_ROWS)
            for j in range(width // MM_COLS):
                cols = slice(j * MM_COLS, (j + 1) * MM_COLS)
                acc_ref[rows, cols] += jnp.dot(a_ref[rows, :], w_ref[:, cols],
                                               preferred_element_type=F32)

    for j in range(n_res):
        @pl.when(k == j)
        def _(j=j):
            cols = slice(j * res_w, (j + 1) * res_w)
            acc_ref[:, cols] += ALPHA * res_ref[...]

    @pl.when(k >= nk)
    def _():
        rows = _ln_rows(k - nk)
        _layer_norm_step(lambda j: acc_ref[rows, j * LN_COLS:(j + 1) * LN_COLS],
                         width // LN_COLS, g_ref, b_ref, o32_ref, obf_ref)


def _mm_acc_res_ln(a, w, layer, res, g, b, *, bm, bk, name):
    M, K = a.shape
    width = w.shape[2]
    nk = K // bk
    ln_steps = bm // LN_ROWS
    n_res = min(nk, 8)
    res_w = width // n_res
    est = 2 * (_nbytes((bm, bk), BF16) + _nbytes((bk, width), BF16) + _nbytes((bm, res_w), F32)
               + _nbytes((LN_ROWS, width), F32) + _nbytes((LN_ROWS, width), BF16)) \
        + _nbytes((bm, width), F32) + 4 * _nbytes((MM_ROWS, MM_COLS), F32)
    last = nk - 1
    return pl.pallas_call(
        functools.partial(_mm_acc_res_ln_body, nk=nk, n_res=n_res),
        grid=(M // bm, nk + ln_steps),
        in_specs=[pl.BlockSpec((bm, bk), lambda m, k: (m, jnp.minimum(k, last))),
                  pl.BlockSpec((None, bk, width), lambda m, k: (layer, jnp.minimum(k, last), 0)),
                  pl.BlockSpec((bm, res_w), lambda m, k: (m, jnp.minimum(k, n_res - 1))),
                  pl.BlockSpec((1, width), lambda m, k: (0, 0)),
                  pl.BlockSpec((1, width), lambda m, k: (0, 0))],
        out_specs=_ln_out_specs(nk, ln_steps, width),
        out_shape=[jax.ShapeDtypeStruct((M, width), F32), jax.ShapeDtypeStruct((M, width), BF16)],
        scratch_shapes=[pltpu.VMEM((bm, width), F32)],
        compiler_params=_compiler_params(2, est), name=name,
    )(a, w, res, g.reshape(1, -1), b.reshape(1, -1))


def _mm_cols_res_ln_body(a_ref, w_ref, res_ref, g_ref, b_ref, o32_ref, obf_ref, acc_ref,
                         *, n_cols):
    n = pl.program_id(1)
    bm = acc_ref.shape[1]

    @pl.when(n < n_cols)
    def _():
        for r in range(bm // MM_ROWS):
            rows = slice(r * MM_ROWS, (r + 1) * MM_ROWS)
            acc_ref[n, rows, :] = ALPHA * res_ref[rows, :] + jnp.dot(
                a_ref[rows, :], w_ref[...], preferred_element_type=F32)

    @pl.when(n >= n_cols)
    def _():
        rows = _ln_rows(n - n_cols)
        _layer_norm_step(lambda j: acc_ref[j, rows, :], n_cols, g_ref, b_ref, o32_ref, obf_ref)


def _mm_cols_res_ln(a, w, layer, res, g, b, *, bm, name):
    M, K = a.shape
    width = w.shape[2]
    bn = LN_COLS
    n_cols = width // bn
    ln_steps = bm // LN_ROWS
    est = 2 * (_nbytes((bm, K), BF16) + _nbytes((K, bn), BF16) + _nbytes((bm, bn), F32)
               + _nbytes((LN_ROWS, width), F32) + _nbytes((LN_ROWS, width), BF16)) \
        + _nbytes((bm, width), F32) + 2 * _nbytes((MM_ROWS, bn), F32)
    last = n_cols - 1
    return pl.pallas_call(
        functools.partial(_mm_cols_res_ln_body, n_cols=n_cols),
        grid=(M // bm, n_cols + ln_steps),
        in_specs=[pl.BlockSpec((bm, K), lambda m, n: (m, 0)),
                  pl.BlockSpec((None, K, bn), lambda m, n: (layer, 0, jnp.minimum(n, last))),
                  pl.BlockSpec((bm, bn), lambda m, n: (m, jnp.minimum(n, last))),
                  pl.BlockSpec((1, width), lambda m, n: (0, 0)),
                  pl.BlockSpec((1, width), lambda m, n: (0, 0))],
        out_specs=_ln_out_specs(n_cols, ln_steps, width),
        out_shape=[jax.ShapeDtypeStruct((M, width), F32), jax.ShapeDtypeStruct((M, width), BF16)],
        scratch_shapes=[pltpu.VMEM((n_cols, bm, bn), F32)],
        compiler_params=_compiler_params(2, est), name=name,
    )(a, w, res, g.reshape(1, -1), b.reshape(1, -1))


def _layer(l, h32, hbf, seq_len, wts, sinks, conf_dw, conf_dw_b, conf_ln_g, conf_ln_b, sc_dw,
           ln1_g, ln1_b, ln2_g, ln2_b):
    w_in_b, pa_b, pb_b, pc_b, w_out_b, w_ffn_in_b, w_ffn_down_b = wts
    (qkv,) = _proj(hbf, w_in_b, l, (OFF_Q,), _qkv_epilogue, (BF16,),
                   n_cols=QKV_W, out_cols=QKV_W, bm=1024, bn=512, name="qkv_proj")
    c, gb, gx = _proj(hbf, w_in_b, l,
                      (OFF_CONF_VAL, OFF_CONF_GATE, OFF_SC_B, OFF_SC_C, OFF_SC_X),
                      _mixer_epilogue, (F32, F32, F32), n_cols=CONF_W, out_cols=CONF_W,
                      bm=1024, bn=256, name="mixer_proj")
    o_a = _attention(qkv, sinks, seq_len)
    o_b, o_c = _conv_mixers(c, gb, gx, conf_dw, conf_dw_b, conf_ln_g, conf_ln_b, sc_dw, seq_len)
    merged = _merge(hbf, o_a, o_b, o_c, w_in_b, pa_b, pb_b, pc_b, l, bm=1024, bn=256)
    x32, xbf = _mm_cols_res_ln(merged, w_out_b, l, h32, ln1_g, ln1_b,
                               bm=1024, name="out_proj_ln")
    (act,) = _proj(xbf, w_ffn_in_b, l, (0, D_FF), _swiglu_epilogue, (BF16,),
                   n_cols=D_FF, out_cols=D_FF_PAD, bm=2048, bn=256, name="ffn_in_swiglu")
    return _mm_acc_res_ln(act, w_ffn_down_b, l, x32, ln2_g, ln2_b,
                          bm=1024, bk=1024, name="ffn_down_ln")


def kernel(x, w_in, attn_sinks, conf_dw, conf_dw_b, conf_ln_g, conf_ln_b, sc_dw, w_proj_attn,
           w_proj_conf, w_proj_sc, w_out, ln1_g, ln1_b, w_ffn_in, w_ffn_down, ln2_g, ln2_b):
    batch, seq_len, d_model = x.shape
    h32 = x.reshape(batch * seq_len, d_model)
    hbf = h32.astype(BF16)
    wts = (w_in.astype(BF16), w_proj_attn.astype(BF16), w_proj_conf.astype(BF16),
           w_proj_sc.astype(BF16), w_out.astype(BF16), w_ffn_in.astype(BF16),
           jnp.pad(w_ffn_down, ((0, 0), (0, D_FF_PAD - D_FF), (0, 0))).astype(BF16))
    for l in range(DEPTH):
        h32, hbf = _layer(l, h32, hbf, seq_len, wts, attn_sinks[l], conf_dw[l], conf_dw_b[l],
                          conf_ln_g[l], conf_ln_b[l], sc_dw[l], ln1_g[l], ln1_b[l],
                          ln2_g[l], ln2_b[l])
    return h32.reshape(batch, seq_len, d_model)
```

```python
import functools

import jax
import jax.numpy as jnp
from jax import lax
from jax.experimental import pallas as pl
from jax.experimental.pallas import tpu as pltpu

D_MODEL = 4096
DEPTH = 2
HEAD_DIM = 64
N_Q_HEADS = 32
N_KV_HEADS = 4
GROUP = 8
BLOCK = 128
ATTN_W = N_Q_HEADS * HEAD_DIM
KV_W = N_KV_HEADS * HEAD_DIM
CONF_W = 1024
CONF_K = 31
SC_W = 1024
SC_K = 3
D_FF = 11008
ALPHA = (2 * DEPTH) ** 0.25
LN_EPS = 1e-5
NEG_INF = -1e30

OFF_Q = 0
OFF_K = ATTN_W
OFF_V = ATTN_W + KV_W
OFF_CONF_VAL = ATTN_W + 2 * KV_W
OFF_CONF_GATE = OFF_CONF_VAL + CONF_W
OFF_SC_B = OFF_CONF_GATE + CONF_W
OFF_SC_C = OFF_SC_B + SC_W
OFF_SC_X = OFF_SC_C + SC_W
OFF_GATE = OFF_SC_X + SC_W
QKV_W = ATTN_W + 2 * KV_W

V7X_VMEM_BYTES = 64 * 1024 * 1024
V7X_VMEM_CAP = V7X_VMEM_BYTES - 4 * 1024 * 1024
LANE = 128
SUBLANE = 8

D_FF_PAD = 11264

F32 = jnp.float32
BF16 = jnp.bfloat16


def _compiler_params(n_grid, vmem_estimate):
    limit = min(V7X_VMEM_CAP, int(vmem_estimate * 1.2) + (8 << 20))
    return pltpu.CompilerParams(
        dimension_semantics=("arbitrary",) * n_grid, vmem_limit_bytes=limit)


def _nbytes(shape, dtype):
    n = 1
    for s in shape:
        n *= s
    return n * jnp.dtype(dtype).itemsize


def _proj_body(*refs, n_dots, epilogue, n_real, n_total):
    x_ref = refs[0]
    w_refs = refs[1:1 + n_dots]
    o_refs = refs[1 + n_dots:]

    def compute():
        x = x_ref[...]
        zs = [jnp.dot(x, w_ref[...], preferred_element_type=F32) for w_ref in w_refs]
        outs = epilogue(*zs)
        for o_ref, o in zip(o_refs, outs):
            o_ref[...] = o.astype(o_ref.dtype)

    if n_real == n_total:
        compute()
        return

    n = pl.program_id(1)
    pl.when(n < n_real)(compute)

    @pl.when(n >= n_real)
    def _():
        for o_ref in o_refs:
            o_ref[...] = jnp.zeros_like(o_ref)


def _col_index(m, n, *, layer, base, last):
    return (layer, 0, base + jnp.minimum(n, last))


def _proj(lhs, w, layer, col_offsets, epilogue, out_dtypes, *, n_cols, out_cols, bm, bn, name):
    M, K = lhs.shape
    n_real = n_cols // bn
    in_specs = [pl.BlockSpec((bm, K), lambda m, n: (m, 0))]
    for off in col_offsets:
        assert off % bn == 0
        in_specs.append(pl.BlockSpec(
            (None, K, bn),
            functools.partial(_col_index, layer=layer, base=off // bn, last=n_real - 1)))
    out_specs = [pl.BlockSpec((bm, bn), lambda m, n: (m, n)) for _ in out_dtypes]
    out_shape = [jax.ShapeDtypeStruct((M, out_cols), dt) for dt in out_dtypes]
    est = 2 * (_nbytes((bm, K), lhs.dtype) + len(col_offsets) * _nbytes((K, bn), w.dtype)
               + sum(_nbytes((bm, bn), dt) for dt in out_dtypes))
    est += (len(col_offsets) + len(out_dtypes)) * _nbytes((bm, bn), F32)
    return pl.pallas_call(
        functools.partial(_proj_body, n_dots=len(col_offsets), epilogue=epilogue, n_real=n_real,
                          n_total=out_cols // bn),
        grid=(M // bm, out_cols // bn),
        in_specs=in_specs, out_specs=out_specs, out_shape=out_shape,
        compiler_params=_compiler_params(2, est), name=name,
    )(lhs, *([w] * len(col_offsets)))


def _qkv_epilogue(z):
    return (z,)


def _mixer_epilogue(c_val, c_gate, g_b, g_c, x_in):
    return (c_val * jax.nn.sigmoid(c_gate), g_b, g_c * x_in)


def _swiglu_epilogue(f_gate, f_up):
    return (jax.nn.silu(f_gate) * f_up,)


def _attn_body(sink_ref, q_ref, kp_ref, kc_ref, vp_ref, vc_ref, bias_ref, o_ref):
    bias = bias_ref[0]
    lo = lax.broadcasted_iota(jnp.int32, (2 * BLOCK, LANE), 1) < HEAD_DIM
    lo_out = lax.broadcasted_iota(jnp.int32, (4 * BLOCK, LANE), 1) < HEAD_DIM
    first_keys = lax.broadcasted_iota(jnp.int32, (4 * BLOCK, LANE), 0) < 2 * BLOCK
    ones_blk = jnp.where(first_keys == lo_out, 1.0, 0.0).astype(BF16)
    scale = HEAD_DIM ** -0.5
    for j in range(N_KV_HEADS // 2):
        lanes = slice(j * LANE, (j + 1) * LANE)
        kcat = jnp.concatenate([kp_ref[:, lanes], kc_ref[:, lanes]], axis=0).astype(F32) * scale
        vcat = jnp.concatenate([vp_ref[:, lanes], vc_ref[:, lanes]], axis=0).astype(F32)
        krot = pltpu.roll(kcat, HEAD_DIM, axis=1)
        vrot = pltpu.roll(vcat, HEAD_DIM, axis=1)
        for hh in range(2):
            h = 2 * j + hh
            k_lo, k_hi = (kcat, krot) if hh == 0 else (krot, kcat)
            v_lo, v_hi = (vcat, vrot) if hh == 0 else (vrot, vcat)
            kk = jnp.concatenate([jnp.where(lo, k_lo, 0.0), jnp.where(lo, 0.0, k_hi)],
                                 axis=0).astype(BF16)
            vv = jnp.concatenate([jnp.where(lo, v_lo, 0.0), jnp.where(lo, 0.0, v_hi)],
                                 axis=0).astype(BF16)
            rhs = jnp.concatenate([vv, ones_blk], axis=1)
            base = h * GROUP * HEAD_DIM
            qs = jnp.concatenate(
                [q_ref[:, base + p * LANE: base + (p + 1) * LANE] for p in range(GROUP // 2)],
                axis=0)
            s = lax.dot_general(qs, kk, (((1,), (1,)), ((), ())),
                                preferred_element_type=F32) + bias
            sink_e = [jnp.concatenate(
                [jnp.full((BLOCK, LANE), sink_ref[h * GROUP + 2 * p + e], F32)
                 for p in range(GROUP // 2)], axis=0) for e in range(2)]
            ps, ms = [], []
            for e in range(2):
                se = s[:, e * 2 * BLOCK:(e + 1) * 2 * BLOCK]
                row_max = jnp.max(se, axis=-1, keepdims=True)
                m = jnp.maximum(jnp.broadcast_to(row_max, (4 * BLOCK, LANE)), sink_e[e])
                ms.append(m)
                ps.append(jnp.exp(se - jnp.concatenate([m, m], axis=1)).astype(BF16))
            pcat = jnp.concatenate(ps, axis=1)
            ov = jnp.dot(pcat, rhs, preferred_element_type=F32)
            m_sel = jnp.where(lo_out, ms[0], ms[1])
            sink_sel = jnp.where(lo_out, sink_e[0], sink_e[1])
            den = ov[:, LANE:] + jnp.exp(sink_sel - m_sel)
            o = ov[:, :LANE] / den
            for p in range(GROUP // 2):
                o_ref[:, base + p * LANE: base + (p + 1) * LANE] = (
                    o[p * BLOCK:(p + 1) * BLOCK].astype(o_ref.dtype))


def _attn_bias():
    r = jnp.arange(4 * BLOCK)[:, None] % BLOCK
    c = jnp.arange(4 * BLOCK)[None, :] % (2 * BLOCK)
    band = (c - r >= 1) & (c - r <= BLOCK)
    first = band & (c >= BLOCK)
    return jnp.where(jnp.stack([band, first]), 0.0, NEG_INF).astype(F32)


def _attention(qkv, sinks, seq_len):
    M = qkv.shape[0]
    blocks_per_seq = seq_len // BLOCK
    k_col = OFF_K // KV_W
    v_col = OFF_V // KV_W

    def prev(i):
        return jnp.maximum(i - 1, 0)

    in_specs = [
        pl.BlockSpec(memory_space=pltpu.SMEM),
        pl.BlockSpec((BLOCK, ATTN_W), lambda i: (i, 0)),
        pl.BlockSpec((BLOCK, KV_W), lambda i: (prev(i), k_col)),
        pl.BlockSpec((BLOCK, KV_W), lambda i: (i, k_col)),
        pl.BlockSpec((BLOCK, KV_W), lambda i: (prev(i), v_col)),
        pl.BlockSpec((BLOCK, KV_W), lambda i: (i, v_col)),
        pl.BlockSpec((1, 4 * BLOCK, 4 * BLOCK),
                     lambda i: (jnp.where(i % blocks_per_seq == 0, 1, 0), 0, 0)),
    ]
    est = 2 * (2 * _nbytes((BLOCK, ATTN_W), BF16) + 4 * _nbytes((BLOCK, KV_W), BF16)
               + _nbytes((4 * BLOCK, 4 * BLOCK), F32)) + 16 * _nbytes((4 * BLOCK, 4 * BLOCK), F32)
    return pl.pallas_call(
        _attn_body,
        grid=(M // BLOCK,),
        in_specs=in_specs,
        out_specs=pl.BlockSpec((BLOCK, ATTN_W), lambda i: (i, 0)),
        out_shape=jax.ShapeDtypeStruct((M, ATTN_W), BF16),
        compiler_params=_compiler_params(1, est), name="swa_attention",
    )(sinks, qkv, qkv, qkv, qkv, qkv, _attn_bias())


CONV_BT = 256
CONF_HALO = 32
SC_HALO = 8
CONV_ROWS = 64
CONV_LANES = 256
CONV_SHIFT_ROWS = CONV_BT + CONF_HALO - SUBLANE


def _conv_body(c_ref, ch_ref, gb_ref, gx_ref, gxh_ref, cw_ref, cb_ref, lg_ref, lb_ref, sw_ref,
               ob_ref, oc_ref, cext_ref, y_ref, gext_ref, shift_ref, *, blocks_per_seq):
    first = (pl.program_id(0) % blocks_per_seq) == 0
    cext_ref[0:CONF_HALO, :] = jnp.where(first, 0.0, ch_ref[...])
    cext_ref[CONF_HALO:, :] = c_ref[...]
    gext_ref[0:SC_HALO, :] = jnp.where(first, 0.0, gxh_ref[...])
    gext_ref[SC_HALO:, :] = gx_ref[...]
    for s in range(1, SUBLANE):
        shift_ref[s - 1] = cext_ref[s:s + CONV_SHIFT_ROWS, :]

    for lc in range(CONF_W // CONV_LANES):
        lanes = slice(lc * CONV_LANES, (lc + 1) * CONV_LANES)
        for rb in range(CONV_BT // CONV_ROWS):
            r0 = rb * CONV_ROWS
            acc = jnp.zeros((CONV_ROWS, CONV_LANES), F32)
            for k in range(CONF_K):
                off = CONF_HALO - (CONF_K - 1) + k
                s, start = off % SUBLANE, r0 + off - off % SUBLANE
                if s == 0:
                    tap = cext_ref[start:start + CONV_ROWS, lanes]
                else:
                    tap = shift_ref[s - 1, start:start + CONV_ROWS, lanes]
                acc = acc + cw_ref[k:k + 1, lanes] * tap
            y_ref[r0:r0 + CONV_ROWS, lanes] = acc + cb_ref[:, lanes]
            acc2 = jnp.zeros((CONV_ROWS, CONV_LANES), F32)
            for k in range(SC_K):
                start = r0 + SC_HALO - (SC_K - 1) + k
                acc2 = acc2 + sw_ref[k:k + 1, lanes] * gext_ref[start:start + CONV_ROWS, lanes]
            oc_ref[r0:r0 + CONV_ROWS, lanes] = (
                gb_ref[r0:r0 + CONV_ROWS, lanes] * acc2).astype(oc_ref.dtype)

    y = y_ref[...]
    mu = jnp.mean(y, axis=-1, keepdims=True)
    yc = y - mu
    var = jnp.mean(yc * yc, axis=-1, keepdims=True)
    yn = yc * lax.rsqrt(var + LN_EPS) * lg_ref[...] + lb_ref[...]
    ob_ref[...] = (yn * jax.nn.sigmoid(yn)).astype(ob_ref.dtype)


def _conv_mixers(c, gb, gx, conf_dw, conf_dw_b, conf_ln_g, conf_ln_b, sc_dw, seq_len):
    M = c.shape[0]
    bt = CONV_BT
    row = lambda i: (i, 0)
    whole = lambda i: (0, 0)
    in_specs = [
        pl.BlockSpec((bt, CONF_W), row),
        pl.BlockSpec((CONF_HALO, CONF_W), lambda i: (jnp.maximum(i * (bt // CONF_HALO) - 1, 0), 0)),
        pl.BlockSpec((bt, SC_W), row),
        pl.BlockSpec((bt, SC_W), row),
        pl.BlockSpec((SC_HALO, SC_W), lambda i: (jnp.maximum(i * (bt // SC_HALO) - 1, 0), 0)),
        pl.BlockSpec((CONF_K, CONF_W), whole),
        pl.BlockSpec((1, CONF_W), whole),
        pl.BlockSpec((1, CONF_W), whole),
        pl.BlockSpec((1, CONF_W), whole),
        pl.BlockSpec((SC_K, SC_W), whole),
    ]
    est = 2 * (3 * _nbytes((bt, CONF_W), F32) + 2 * _nbytes((bt, CONF_W), BF16)) \
        + 6 * _nbytes((bt + CONF_HALO, CONF_W), F32) \
        + _nbytes((SUBLANE - 1, CONV_SHIFT_ROWS, CONF_W), F32)
    return pl.pallas_call(
        functools.partial(_conv_body, blocks_per_seq=seq_len // bt),
        grid=(M // bt,),
        in_specs=in_specs,
        out_specs=[pl.BlockSpec((bt, CONF_W), row), pl.BlockSpec((bt, SC_W), row)],
        out_shape=[jax.ShapeDtypeStruct((M, CONF_W), BF16), jax.ShapeDtypeStruct((M, SC_W), BF16)],
        scratch_shapes=[pltpu.VMEM((bt + CONF_HALO, CONF_W), F32),
                        pltpu.VMEM((bt, CONF_W), F32),
                        pltpu.VMEM((bt + SC_HALO, SC_W), F32),
                        pltpu.VMEM((SUBLANE - 1, CONV_SHIFT_ROWS, CONF_W), F32)],
        compiler_params=_compiler_params(1, est), name="conv_mixers",
    )(c, c, gb, gx, gx, conf_dw, conf_dw_b.reshape(1, -1), conf_ln_g.reshape(1, -1),
      conf_ln_b.reshape(1, -1), sc_dw)


MERGE_ROWS = 512


def _merge_body(x_ref, oa_ref, ob_ref, oc_ref, wg0_ref, wg1_ref, wg2_ref,
                pa_ref, pb_ref, pc_ref, out_ref):
    for r in range(out_ref.shape[0] // MERGE_ROWS):
        rows = slice(r * MERGE_ROWS, (r + 1) * MERGE_ROWS)
        acc = None
        for o_ref, wg_ref, p_ref in ((oa_ref, wg0_ref, pa_ref), (ob_ref, wg1_ref, pb_ref),
                                     (oc_ref, wg2_ref, pc_ref)):
            gate = jax.nn.sigmoid(
                jnp.dot(x_ref[rows, :], wg_ref[...], preferred_element_type=F32))
            term = gate * jnp.dot(o_ref[rows, :], p_ref[...], preferred_element_type=F32)
            acc = term if acc is None else acc + term
        out_ref[rows, :] = acc.astype(out_ref.dtype)


def _merge(xb, oa, ob, oc, w_in_b, pa, pb, pc, layer, *, bm, bn):
    M = xb.shape[0]
    n_blocks = D_MODEL // bn
    lhs_spec = lambda a: pl.BlockSpec((bm, a.shape[1]), lambda m, n: (m, 0))
    gate_spec = lambda i: pl.BlockSpec(
        (None, D_MODEL, bn),
        functools.partial(_col_index, layer=layer, base=(OFF_GATE + i * D_MODEL) // bn,
                          last=n_blocks - 1))
    p_spec = lambda p: pl.BlockSpec((None, p.shape[1], bn), lambda m, n: (layer, 0, n))
    k_total = D_MODEL + ATTN_W + CONF_W + SC_W
    est = 2 * (_nbytes((bm, k_total), BF16) + _nbytes((3 * D_MODEL + k_total - D_MODEL, bn), BF16)
               + _nbytes((bm, bn), BF16)) + 8 * _nbytes((bm, bn), F32)
    return pl.pallas_call(
        _merge_body,
        grid=(M // bm, D_MODEL // bn),
        in_specs=[lhs_spec(xb), lhs_spec(oa), lhs_spec(ob), lhs_spec(oc),
                  gate_spec(0), gate_spec(1), gate_spec(2), p_spec(pa), p_spec(pb), p_spec(pc)],
        out_specs=pl.BlockSpec((bm, bn), lambda m, n: (m, n)),
        out_shape=jax.ShapeDtypeStruct((M, D_MODEL), BF16),
        compiler_params=_compiler_params(2, est), name="gated_merge",
    )(xb, oa, ob, oc, w_in_b, w_in_b, w_in_b, pa, pb, pc)


MM_COLS = 512


MM_ROWS = 512
LN_ROWS = 256
LN_CHUNK = 128
LN_COLS = MM_COLS


def _layer_norm_step(load_piece, step, n_pieces, g_ref, b_ref, out_refs):
    width = n_pieces * LN_COLS
    for i in range(LN_ROWS // LN_CHUNK):
        rows = pl.ds(pl.multiple_of(step * LN_ROWS + i * LN_CHUNK, LN_CHUNK), LN_CHUNK)
        out_rows = slice(i * LN_CHUNK, (i + 1) * LN_CHUNK)
        total = None
        for j in range(n_pieces):
            s = jnp.sum(load_piece(j, rows), axis=-1, keepdims=True)
            total = s if total is None else total + s
        mu = total / width
        total = None
        for j in range(n_pieces):
            d = load_piece(j, rows) - mu
            s = jnp.sum(d * d, axis=-1, keepdims=True)
            total = s if total is None else total + s
        rstd = lax.rsqrt(total / width + LN_EPS)
        for j in range(n_pieces):
            cols = slice(j * LN_COLS, (j + 1) * LN_COLS)
            out = (load_piece(j, rows) - mu) * rstd * g_ref[:, cols] + b_ref[:, cols]
            for o_ref in out_refs:
                o_ref[out_rows, cols] = out.astype(o_ref.dtype)


def _ln_prefetch_maps(n_main, n_row_blocks):
    def row_blk(m, s):
        return jnp.where(s >= n_main, jnp.minimum(m + 1, n_row_blocks - 1), m)

    def col_blk(s):
        return jnp.where(s >= n_main, 0, s)

    return row_blk, col_blk


def _ln_outputs(n_main, ln_steps, n_rows, width, out_dtypes):
    index = lambda m, s: (m * ln_steps + jnp.maximum(s - n_main, 0), 0)
    specs = [pl.BlockSpec((LN_ROWS, width), index) for _ in out_dtypes]
    shapes = [jax.ShapeDtypeStruct((n_rows, width), dt) for dt in out_dtypes]
    vmem = 2 * sum(_nbytes((LN_ROWS, width), dt) for dt in out_dtypes)
    return specs, shapes, vmem


def _mm_acc_res_ln_body(a_ref, w_ref, res_ref, g_ref, b_ref, *rest, nk, n_res):
    out_refs, acc_ref = rest[:-1], rest[-1]
    k = pl.program_id(1)
    bm, width = acc_ref.shape
    res_w = width // n_res

    @pl.when(k == 0)
    def _():
        acc_ref[...] = jnp.zeros_like(acc_ref)

    @pl.when(k < nk)
    def _():
        for r in range(bm // MM_ROWS):
            rows = slice(r * MM_ROWS, (r + 1) * MM_ROWS)
            for j in range(width // MM_COLS):
                cols = slice(j * MM_COLS, (j + 1) * MM_COLS)
                acc_ref[rows, cols] += jnp.dot(a_ref[rows, :], w_ref[:, cols],
                                               preferred_element_type=F32)

    for j in range(n_res):
        @pl.when(k == j)
        def _(j=j):
            cols = slice(j * res_w, (j + 1) * res_w)
            acc_ref[:, cols] += ALPHA * res_ref[...]

    @pl.when(k >= nk)
    def _():
        _layer_norm_step(lambda j, rows: acc_ref[rows, j * LN_COLS:(j + 1) * LN_COLS],
                         k - nk, width // LN_COLS, g_ref, b_ref, out_refs)


def _mm_acc_res_ln(a, w, layer, res, g, b, out_dtypes, *, bm, bk, name):
    M, K = a.shape
    width = w.shape[2]
    nk = K // bk
    ln_steps = bm // LN_ROWS
    n_res = min(nk, 8)
    res_w = width // n_res
    out_specs, out_shape, out_vmem = _ln_outputs(nk, ln_steps, M, width, out_dtypes)
    est = 2 * (_nbytes((bm, bk), BF16) + _nbytes((bk, width), BF16) + _nbytes((bm, res_w), F32)) \
        + out_vmem + _nbytes((bm, width), F32) + 4 * _nbytes((MM_ROWS, MM_COLS), F32)
    row_blk, col_blk = _ln_prefetch_maps(nk, M // bm)
    return pl.pallas_call(
        functools.partial(_mm_acc_res_ln_body, nk=nk, n_res=n_res),
        grid=(M // bm, nk + ln_steps),
        in_specs=[pl.BlockSpec((bm, bk), lambda m, k: (row_blk(m, k), col_blk(k))),
                  pl.BlockSpec((None, bk, width), lambda m, k: (layer, col_blk(k), 0)),
                  pl.BlockSpec((bm, res_w),
                               lambda m, k: (row_blk(m, k), jnp.minimum(col_blk(k), n_res - 1))),
                  pl.BlockSpec((1, width), lambda m, k: (0, 0)),
                  pl.BlockSpec((1, width), lambda m, k: (0, 0))],
        out_specs=out_specs, out_shape=out_shape,
        scratch_shapes=[pltpu.VMEM((bm, width), F32)],
        compiler_params=_compiler_params(2, est), name=name,
    )(a, w, res, g.reshape(1, -1), b.reshape(1, -1))


def _mm_cols_res_ln_body(a_ref, w_ref, res_ref, g_ref, b_ref, *rest, n_cols):
    out_refs, acc_ref = rest[:-1], rest[-1]
    n = pl.program_id(1)
    bm = acc_ref.shape[1]

    @pl.when(n < n_cols)
    def _():
        for r in range(bm // MM_ROWS):
            rows = slice(r * MM_ROWS, (r + 1) * MM_ROWS)
            acc_ref[n, rows, :] = ALPHA * res_ref[rows, :] + jnp.dot(
                a_ref[rows, :], w_ref[...], preferred_element_type=F32)

    @pl.when(n >= n_cols)
    def _():
        _layer_norm_step(lambda j, rows: acc_ref[j, rows, :], n - n_cols, n_cols,
                         g_ref, b_ref, out_refs)


def _mm_cols_res_ln(a, w, layer, res, g, b, out_dtypes, *, bm, name):
    M, K = a.shape
    width = w.shape[2]
    bn = LN_COLS
    n_cols = width // bn
    ln_steps = bm // LN_ROWS
    out_specs, out_shape, out_vmem = _ln_outputs(n_cols, ln_steps, M, width, out_dtypes)
    est = 2 * (_nbytes((bm, K), BF16) + _nbytes((K, bn), BF16) + _nbytes((bm, bn), F32)) \
        + out_vmem + _nbytes((bm, width), F32) + 2 * _nbytes((MM_ROWS, bn), F32)
    row_blk, col_blk = _ln_prefetch_maps(n_cols, M // bm)
    return pl.pallas_call(
        functools.partial(_mm_cols_res_ln_body, n_cols=n_cols),
        grid=(M // bm, n_cols + ln_steps),
        in_specs=[pl.BlockSpec((bm, K), lambda m, n: (row_blk(m, n), 0)),
                  pl.BlockSpec((None, K, bn), lambda m, n: (layer, 0, col_blk(n))),
                  pl.BlockSpec((bm, bn), lambda m, n: (row_blk(m, n), col_blk(n))),
                  pl.BlockSpec((1, width), lambda m, n: (0, 0)),
                  pl.BlockSpec((1, width), lambda m, n: (0, 0))],
        out_specs=out_specs, out_shape=out_shape,
        scratch_shapes=[pltpu.VMEM((n_cols, bm, bn), F32)],
        compiler_params=_compiler_params(2, est), name=name,
    )(a, w, res, g.reshape(1, -1), b.reshape(1, -1))


CAST_ROWS = 512


def _cast_pad_rows_body(w_ref, o_ref, *, rows_in):
    row = pl.program_id(1) * CAST_ROWS + lax.broadcasted_iota(jnp.int32, w_ref.shape, 0)
    o_ref[...] = jnp.where(row < rows_in, w_ref[...], 0.0).astype(o_ref.dtype)


def _cast_pad_rows(w, rows_out):
    depth, rows_in, cols = w.shape
    spec = pl.BlockSpec((None, CAST_ROWS, cols), lambda l, r: (l, r, 0))
    est = 2 * (_nbytes((CAST_ROWS, cols), F32) + _nbytes((CAST_ROWS, cols), BF16))
    return pl.pallas_call(
        functools.partial(_cast_pad_rows_body, rows_in=rows_in),
        grid=(depth, rows_out // CAST_ROWS),
        in_specs=[spec], out_specs=spec,
        out_shape=jax.ShapeDtypeStruct((depth, rows_out, cols), BF16),
        compiler_params=_compiler_params(2, est), name="cast_pad_rows",
    )(w)


def _layer(l, h32, hbf, seq_len, wts, sinks, conf_dw, conf_dw_b, conf_ln_g, conf_ln_b, sc_dw,
           ln1_g, ln1_b, ln2_g, ln2_b):
    w_in_b, pa_b, pb_b, pc_b, w_out_b, w_ffn_in_b, w_ffn_down_b = wts
    (qkv,) = _proj(hbf, w_in_b, l, (OFF_Q,), _qkv_epilogue, (BF16,),
                   n_cols=QKV_W, out_cols=QKV_W, bm=1024, bn=512, name="qkv_proj")
    c, gb, gx = _proj(hbf, w_in_b, l,
                      (OFF_CONF_VAL, OFF_CONF_GATE, OFF_SC_B, OFF_SC_C, OFF_SC_X),
                      _mixer_epilogue, (F32, F32, F32), n_cols=CONF_W, out_cols=CONF_W,
                      bm=1024, bn=256, name="mixer_proj")
    o_a = _attention(qkv, sinks, seq_len)
    o_b, o_c = _conv_mixers(c, gb, gx, conf_dw, conf_dw_b, conf_ln_g, conf_ln_b, sc_dw, seq_len)
    merged = _merge(hbf, o_a, o_b, o_c, w_in_b, pa_b, pb_b, pc_b, l, bm=1024, bn=256)
    x32, xbf = _mm_cols_res_ln(merged, w_out_b, l, h32, ln1_g, ln1_b, (F32, BF16),
                               bm=1024, name="out_proj_ln")
    (act,) = _proj(xbf, w_ffn_in_b, l, (0, D_FF), _swiglu_epilogue, (BF16,),
                   n_cols=D_FF, out_cols=D_FF_PAD, bm=2048, bn=256, name="ffn_in_swiglu")
    out_dtypes = (F32,) if l == DEPTH - 1 else (F32, BF16)
    outs = _mm_acc_res_ln(act, w_ffn_down_b, l, x32, ln2_g, ln2_b, out_dtypes,
                          bm=1024, bk=1024, name="ffn_down_ln")
    return outs[0], outs[-1]


def kernel(x, w_in, attn_sinks, conf_dw, conf_dw_b, conf_ln_g, conf_ln_b, sc_dw, w_proj_attn,
           w_proj_conf, w_proj_sc, w_out, ln1_g, ln1_b, w_ffn_in, w_ffn_down, ln2_g, ln2_b):
    batch, seq_len, d_model = x.shape
    h32 = x.reshape(batch * seq_len, d_model)
    hbf = h32.astype(BF16)
    wts = (w_in.astype(BF16), w_proj_attn.astype(BF16), w_proj_conf.astype(BF16),
           w_proj_sc.astype(BF16), w_out.astype(BF16), w_ffn_in.astype(BF16),
           _cast_pad_rows(w_ffn_down, D_FF_PAD))
    for l in range(DEPTH):
        h32, hbf = _layer(l, h32, hbf, seq_len, wts, attn_sinks[l], conf_dw[l], conf_dw_b[l],
                          conf_ln_g[l], conf_ln_b[l], sc_dw[l], ln1_g[l], ln1_b[l],
                          ln2_g[l], ln2_b[l])
    return h32.reshape(batch, seq_len, d_model)
```

```python
import functools

import jax
import jax.numpy as jnp
from jax import lax
from jax.experimental import pallas as pl
from jax.experimental.pallas import tpu as pltpu

D_MODEL = 4096
DEPTH = 2
HEAD_DIM = 64
N_Q_HEADS = 32
N_KV_HEADS = 4
GROUP = 8
BLOCK = 128
ATTN_W = N_Q_HEADS * HEAD_DIM
KV_W = N_KV_HEADS * HEAD_DIM
CONF_W = 1024
CONF_K = 31
SC_W = 1024
SC_K = 3
D_FF = 11008
ALPHA = (2 * DEPTH) ** 0.25
LN_EPS = 1e-5
NEG_INF = -1e30

OFF_Q = 0
OFF_K = ATTN_W
OFF_V = ATTN_W + KV_W
OFF_CONF_VAL = ATTN_W + 2 * KV_W
OFF_CONF_GATE = OFF_CONF_VAL + CONF_W
OFF_SC_B = OFF_CONF_GATE + CONF_W
OFF_SC_C = OFF_SC_B + SC_W
OFF_SC_X = OFF_SC_C + SC_W
OFF_GATE = OFF_SC_X + SC_W
QKV_W = ATTN_W + 2 * KV_W

V7X_VMEM_BYTES = 64 * 1024 * 1024
V7X_VMEM_CAP = V7X_VMEM_BYTES - 4 * 1024 * 1024
LANE = 128
SUBLANE = 8

D_FF_PAD = 11264

F32 = jnp.float32
BF16 = jnp.bfloat16


def _compiler_params(n_grid, vmem_estimate):
    limit = min(V7X_VMEM_CAP, int(vmem_estimate * 1.2) + (8 << 20))
    return pltpu.CompilerParams(
        dimension_semantics=("arbitrary",) * n_grid, vmem_limit_bytes=limit)


def _nbytes(shape, dtype):
    n = 1
    for s in shape:
        n *= s
    return n * jnp.dtype(dtype).itemsize


PROJ_ROWS = 512


def _proj_body(*refs, n_dots, epilogue, n_real, n_total):
    x_ref = refs[0]
    w_refs = refs[1:1 + n_dots]
    o_refs = refs[1 + n_dots:]

    def compute():
        for r in range(x_ref.shape[0] // PROJ_ROWS):
            rows = slice(r * PROJ_ROWS, (r + 1) * PROJ_ROWS)
            x = x_ref[rows, :]
            zs = [jnp.dot(x, w_ref[...], preferred_element_type=F32) for w_ref in w_refs]
            outs = epilogue(*zs)
            for o_ref, o in zip(o_refs, outs):
                o_ref[rows, :] = o.astype(o_ref.dtype)

    if n_real == n_total:
        compute()
        return

    n = pl.program_id(1)
    pl.when(n < n_real)(compute)

    @pl.when(n >= n_real)
    def _():
        for o_ref in o_refs:
            o_ref[...] = jnp.zeros_like(o_ref)


def _col_index(m, n, *, layer, base, last):
    return (layer, 0, base + jnp.minimum(n, last))


def _proj(lhs, w, layer, col_offsets, epilogue, out_dtypes, *, n_cols, out_cols, bm, bn, name):
    M, K = lhs.shape
    n_real = n_cols // bn
    last_m = M // bm - 1
    in_specs = [pl.BlockSpec(
        (bm, K), lambda m, n: (jnp.where(n >= n_real, jnp.minimum(m + 1, last_m), m), 0))]
    for off in col_offsets:
        assert off % bn == 0
        in_specs.append(pl.BlockSpec(
            (None, K, bn),
            functools.partial(_col_index, layer=layer, base=off // bn, last=n_real - 1)))
    out_specs = [pl.BlockSpec((bm, bn), lambda m, n: (m, n)) for _ in out_dtypes]
    out_shape = [jax.ShapeDtypeStruct((M, out_cols), dt) for dt in out_dtypes]
    est = 2 * (_nbytes((bm, K), lhs.dtype) + len(col_offsets) * _nbytes((K, bn), w.dtype)
               + sum(_nbytes((bm, bn), dt) for dt in out_dtypes))
    est += (len(col_offsets) + len(out_dtypes)) * _nbytes((bm, bn), F32)
    return pl.pallas_call(
        functools.partial(_proj_body, n_dots=len(col_offsets), epilogue=epilogue, n_real=n_real,
                          n_total=out_cols // bn),
        grid=(M // bm, out_cols // bn),
        in_specs=in_specs, out_specs=out_specs, out_shape=out_shape,
        compiler_params=_compiler_params(2, est), name=name,
    )(lhs, *([w] * len(col_offsets)))


def _qkv_epilogue(z):
    return (z,)


def _mixer_epilogue(c_val, c_gate, g_b, g_c, x_in):
    return (c_val * jax.nn.sigmoid(c_gate), g_b, g_c * x_in)


def _swiglu_epilogue(f_gate, f_up):
    return (jax.nn.silu(f_gate) * f_up,)


def _attn_body(sink_ref, q_ref, kp_ref, kc_ref, vp_ref, vc_ref, bias_ref, o_ref):
    bias = bias_ref[0]
    lo = lax.broadcasted_iota(jnp.int32, (2 * BLOCK, LANE), 1) < HEAD_DIM
    lo_out = lax.broadcasted_iota(jnp.int32, (4 * BLOCK, LANE), 1) < HEAD_DIM
    first_keys = lax.broadcasted_iota(jnp.int32, (4 * BLOCK, LANE), 0) < 2 * BLOCK
    ones_blk = jnp.where(first_keys == lo_out, 1.0, 0.0).astype(BF16)
    scale = HEAD_DIM ** -0.5
    for j in range(N_KV_HEADS // 2):
        lanes = slice(j * LANE, (j + 1) * LANE)
        kcat = jnp.concatenate([kp_ref[:, lanes], kc_ref[:, lanes]], axis=0).astype(F32) * scale
        vcat = jnp.concatenate([vp_ref[:, lanes], vc_ref[:, lanes]], axis=0).astype(F32)
        krot = pltpu.roll(kcat, HEAD_DIM, axis=1)
        vrot = pltpu.roll(vcat, HEAD_DIM, axis=1)
        for hh in range(2):
            h = 2 * j + hh
            k_lo, k_hi = (kcat, krot) if hh == 0 else (krot, kcat)
            v_lo, v_hi = (vcat, vrot) if hh == 0 else (vrot, vcat)
            kk = jnp.concatenate([jnp.where(lo, k_lo, 0.0), jnp.where(lo, 0.0, k_hi)],
                                 axis=0).astype(BF16)
            vv = jnp.concatenate([jnp.where(lo, v_lo, 0.0), jnp.where(lo, 0.0, v_hi)],
                                 axis=0).astype(BF16)
            rhs = jnp.concatenate([vv, ones_blk], axis=1)
            base = h * GROUP * HEAD_DIM
            qs = jnp.concatenate(
                [q_ref[:, base + p * LANE: base + (p + 1) * LANE] for p in range(GROUP // 2)],
                axis=0)
            s = lax.dot_general(qs, kk, (((1,), (1,)), ((), ())),
                                preferred_element_type=F32) + bias
            sink_e = [jnp.concatenate(
                [jnp.full((BLOCK, LANE), sink_ref[h * GROUP + 2 * p + e], F32)
                 for p in range(GROUP // 2)], axis=0) for e in range(2)]
            ps, ms = [], []
            for e in range(2):
                se = s[:, e * 2 * BLOCK:(e + 1) * 2 * BLOCK]
                row_max = jnp.max(se, axis=-1, keepdims=True)
                m = jnp.maximum(jnp.broadcast_to(row_max, (4 * BLOCK, LANE)), sink_e[e])
                ms.append(m)
                ps.append(jnp.exp(se - jnp.concatenate([m, m], axis=1)).astype(BF16))
            pcat = jnp.concatenate(ps, axis=1)
            ov = jnp.dot(pcat, rhs, preferred_element_type=F32)
            m_sel = jnp.where(lo_out, ms[0], ms[1])
            sink_sel = jnp.where(lo_out, sink_e[0], sink_e[1])
            den = ov[:, LANE:] + jnp.exp(sink_sel - m_sel)
            o = ov[:, :LANE] / den
            for p in range(GROUP // 2):
                o_ref[:, base + p * LANE: base + (p + 1) * LANE] = (
                    o[p * BLOCK:(p + 1) * BLOCK].astype(o_ref.dtype))


def _attn_bias():
    r = jnp.arange(4 * BLOCK)[:, None] % BLOCK
    c = jnp.arange(4 * BLOCK)[None, :] % (2 * BLOCK)
    band = (c - r >= 1) & (c - r <= BLOCK)
    first = band & (c >= BLOCK)
    return jnp.where(jnp.stack([band, first]), 0.0, NEG_INF).astype(F32)


def _attention(qkv, sinks, seq_len):
    M = qkv.shape[0]
    blocks_per_seq = seq_len // BLOCK
    k_col = OFF_K // KV_W
    v_col = OFF_V // KV_W

    def prev(i):
        return jnp.maximum(i - 1, 0)

    in_specs = [
        pl.BlockSpec(memory_space=pltpu.SMEM),
        pl.BlockSpec((BLOCK, ATTN_W), lambda i: (i, 0)),
        pl.BlockSpec((BLOCK, KV_W), lambda i: (prev(i), k_col)),
        pl.BlockSpec((BLOCK, KV_W), lambda i: (i, k_col)),
        pl.BlockSpec((BLOCK, KV_W), lambda i: (prev(i), v_col)),
        pl.BlockSpec((BLOCK, KV_W), lambda i: (i, v_col)),
        pl.BlockSpec((1, 4 * BLOCK, 4 * BLOCK),
                     lambda i: (jnp.where(i % blocks_per_seq == 0, 1, 0), 0, 0)),
    ]
    est = 2 * (2 * _nbytes((BLOCK, ATTN_W), BF16) + 4 * _nbytes((BLOCK, KV_W), BF16)
               + _nbytes((4 * BLOCK, 4 * BLOCK), F32)) + 16 * _nbytes((4 * BLOCK, 4 * BLOCK), F32)
    return pl.pallas_call(
        _attn_body,
        grid=(M // BLOCK,),
        in_specs=in_specs,
        out_specs=pl.BlockSpec((BLOCK, ATTN_W), lambda i: (i, 0)),
        out_shape=jax.ShapeDtypeStruct((M, ATTN_W), BF16),
        compiler_params=_compiler_params(1, est), name="swa_attention",
    )(sinks, qkv, qkv, qkv, qkv, qkv, _attn_bias())


CONV_BT = 256
CONF_HALO = 32
SC_HALO = 8
CONV_ROWS = 64
CONV_LANES = 256
CONV_SHIFT_ROWS = CONV_BT + CONF_HALO - SUBLANE


def _conv_body(c_ref, ch_ref, gb_ref, gx_ref, gxh_ref, cw_ref, cb_ref, lg_ref, lb_ref, sw_ref,
               ob_ref, oc_ref, cext_ref, y_ref, gext_ref, shift_ref, *, blocks_per_seq):
    first = (pl.program_id(0) % blocks_per_seq) == 0
    cext_ref[0:CONF_HALO, :] = jnp.where(first, 0.0, ch_ref[...])
    cext_ref[CONF_HALO:, :] = c_ref[...]
    gext_ref[0:SC_HALO, :] = jnp.where(first, 0.0, gxh_ref[...])
    gext_ref[SC_HALO:, :] = gx_ref[...]
    for s in range(1, SUBLANE):
        shift_ref[s - 1] = cext_ref[s:s + CONV_SHIFT_ROWS, :]

    for lc in range(CONF_W // CONV_LANES):
        lanes = slice(lc * CONV_LANES, (lc + 1) * CONV_LANES)
        for rb in range(CONV_BT // CONV_ROWS):
            r0 = rb * CONV_ROWS
            acc = jnp.zeros((CONV_ROWS, CONV_LANES), F32)
            for k in range(CONF_K):
                off = CONF_HALO - (CONF_K - 1) + k
                s, start = off % SUBLANE, r0 + off - off % SUBLANE
                if s == 0:
                    tap = cext_ref[start:start + CONV_ROWS, lanes]
                else:
                    tap = shift_ref[s - 1, start:start + CONV_ROWS, lanes]
                acc = acc + cw_ref[k:k + 1, lanes] * tap
            y_ref[r0:r0 + CONV_ROWS, lanes] = acc + cb_ref[:, lanes]
            acc2 = jnp.zeros((CONV_ROWS, CONV_LANES), F32)
            for k in range(SC_K):
                start = r0 + SC_HALO - (SC_K - 1) + k
                acc2 = acc2 + sw_ref[k:k + 1, lanes] * gext_ref[start:start + CONV_ROWS, lanes]
            oc_ref[r0:r0 + CONV_ROWS, lanes] = (
                gb_ref[r0:r0 + CONV_ROWS, lanes] * acc2).astype(oc_ref.dtype)

    y = y_ref[...]
    mu = jnp.mean(y, axis=-1, keepdims=True)
    yc = y - mu
    var = jnp.mean(yc * yc, axis=-1, keepdims=True)
    yn = yc * lax.rsqrt(var + LN_EPS) * lg_ref[...] + lb_ref[...]
    ob_ref[...] = (yn * jax.nn.sigmoid(yn)).astype(ob_ref.dtype)


def _conv_mixers(c, gb, gx, conf_dw, conf_dw_b, conf_ln_g, conf_ln_b, sc_dw, seq_len):
    M = c.shape[0]
    bt = CONV_BT
    row = lambda i: (i, 0)
    whole = lambda i: (0, 0)
    in_specs = [
        pl.BlockSpec((bt, CONF_W), row),
        pl.BlockSpec((CONF_HALO, CONF_W), lambda i: (jnp.maximum(i * (bt // CONF_HALO) - 1, 0), 0)),
        pl.BlockSpec((bt, SC_W), row),
        pl.BlockSpec((bt, SC_W), row),
        pl.BlockSpec((SC_HALO, SC_W), lambda i: (jnp.maximum(i * (bt // SC_HALO) - 1, 0), 0)),
        pl.BlockSpec((CONF_K, CONF_W), whole),
        pl.BlockSpec((1, CONF_W), whole),
        pl.BlockSpec((1, CONF_W), whole),
        pl.BlockSpec((1, CONF_W), whole),
        pl.BlockSpec((SC_K, SC_W), whole),
    ]
    est = 2 * (3 * _nbytes((bt, CONF_W), F32) + 2 * _nbytes((bt, CONF_W), BF16)) \
        + 6 * _nbytes((bt + CONF_HALO, CONF_W), F32) \
        + _nbytes((SUBLANE - 1, CONV_SHIFT_ROWS, CONF_W), F32)
    return pl.pallas_call(
        functools.partial(_conv_body, blocks_per_seq=seq_len // bt),
        grid=(M // bt,),
        in_specs=in_specs,
        out_specs=[pl.BlockSpec((bt, CONF_W), row), pl.BlockSpec((bt, SC_W), row)],
        out_shape=[jax.ShapeDtypeStruct((M, CONF_W), BF16), jax.ShapeDtypeStruct((M, SC_W), BF16)],
        scratch_shapes=[pltpu.VMEM((bt + CONF_HALO, CONF_W), F32),
                        pltpu.VMEM((bt, CONF_W), F32),
                        pltpu.VMEM((bt + SC_HALO, SC_W), F32),
                        pltpu.VMEM((SUBLANE - 1, CONV_SHIFT_ROWS, CONF_W), F32)],
        compiler_params=_compiler_params(1, est), name="conv_mixers",
    )(c, c, gb, gx, gx, conf_dw, conf_dw_b.reshape(1, -1), conf_ln_g.reshape(1, -1),
      conf_ln_b.reshape(1, -1), sc_dw)


MERGE_ROWS = 512


def _merge_body(x_ref, oa_ref, ob_ref, oc_ref, wg0_ref, wg1_ref, wg2_ref,
                pa_ref, pb_ref, pc_ref, out_ref):
    for r in range(out_ref.shape[0] // MERGE_ROWS):
        rows = slice(r * MERGE_ROWS, (r + 1) * MERGE_ROWS)
        acc = None
        for o_ref, wg_ref, p_ref in ((oa_ref, wg0_ref, pa_ref), (ob_ref, wg1_ref, pb_ref),
                                     (oc_ref, wg2_ref, pc_ref)):
            gate = jax.nn.sigmoid(
                jnp.dot(x_ref[rows, :], wg_ref[...], preferred_element_type=F32))
            term = gate * jnp.dot(o_ref[rows, :], p_ref[...], preferred_element_type=F32)
            acc = term if acc is None else acc + term
        out_ref[rows, :] = acc.astype(out_ref.dtype)


def _merge(xb, oa, ob, oc, w_in_b, pa, pb, pc, layer, *, bm, bn):
    M = xb.shape[0]
    n_blocks = D_MODEL // bn
    lhs_spec = lambda a: pl.BlockSpec((bm, a.shape[1]), lambda m, n: (m, 0))
    gate_spec = lambda i: pl.BlockSpec(
        (None, D_MODEL, bn),
        functools.partial(_col_index, layer=layer, base=(OFF_GATE + i * D_MODEL) // bn,
                          last=n_blocks - 1))
    p_spec = lambda p: pl.BlockSpec((None, p.shape[1], bn), lambda m, n: (layer, 0, n))
    k_total = D_MODEL + ATTN_W + CONF_W + SC_W
    est = 2 * (_nbytes((bm, k_total), BF16) + _nbytes((3 * D_MODEL + k_total - D_MODEL, bn), BF16)
               + _nbytes((bm, bn), BF16)) + 8 * _nbytes((bm, bn), F32)
    return pl.pallas_call(
        _merge_body,
        grid=(M // bm, D_MODEL // bn),
        in_specs=[lhs_spec(xb), lhs_spec(oa), lhs_spec(ob), lhs_spec(oc),
                  gate_spec(0), gate_spec(1), gate_spec(2), p_spec(pa), p_spec(pb), p_spec(pc)],
        out_specs=pl.BlockSpec((bm, bn), lambda m, n: (m, n)),
        out_shape=jax.ShapeDtypeStruct((M, D_MODEL), BF16),
        compiler_params=_compiler_params(2, est), name="gated_merge",
    )(xb, oa, ob, oc, w_in_b, w_in_b, w_in_b, pa, pb, pc)


MM_COLS = 512


MM_ROWS = 512
LN_ROWS = 256
LN_CHUNK = 128
LN_COLS = MM_COLS


def _layer_norm_step(load_piece, step, n_pieces, g_ref, b_ref, out_refs):
    width = n_pieces * LN_COLS
    for i in range(LN_ROWS // LN_CHUNK):
        rows = pl.ds(pl.multiple_of(step * LN_ROWS + i * LN_CHUNK, LN_CHUNK), LN_CHUNK)
        out_rows = slice(i * LN_CHUNK, (i + 1) * LN_CHUNK)
        total = None
        for j in range(n_pieces):
            s = jnp.sum(load_piece(j, rows), axis=-1, keepdims=True)
            total = s if total is None else total + s
        mu = total / width
        total = None
        for j in range(n_pieces):
            d = load_piece(j, rows) - mu
            s = jnp.sum(d * d, axis=-1, keepdims=True)
            total = s if total is None else total + s
        rstd = lax.rsqrt(total / width + LN_EPS)
        for j in range(n_pieces):
            cols = slice(j * LN_COLS, (j + 1) * LN_COLS)
            out = (load_piece(j, rows) - mu) * rstd * g_ref[:, cols] + b_ref[:, cols]
            for o_ref in out_refs:
                o_ref[out_rows, cols] = out.astype(o_ref.dtype)


def _ln_prefetch_maps(n_main, n_row_blocks):
    def row_blk(m, s):
        return jnp.where(s >= n_main, jnp.minimum(m + 1, n_row_blocks - 1), m)

    def col_blk(s):
        return jnp.where(s >= n_main, 0, s)

    return row_blk, col_blk


def _ln_outputs(n_main, ln_steps, n_rows, width, out_dtypes):
    index = lambda m, s: (m * ln_steps + jnp.maximum(s - n_main, 0), 0)
    specs = [pl.BlockSpec((LN_ROWS, width), index) for _ in out_dtypes]
    shapes = [jax.ShapeDtypeStruct((n_rows, width), dt) for dt in out_dtypes]
    vmem = 2 * sum(_nbytes((LN_ROWS, width), dt) for dt in out_dtypes)
    return specs, shapes, vmem


def _mm_acc_res_ln_body(a_ref, w_ref, res_ref, g_ref, b_ref, *rest, nk, n_res):
    out_refs, acc_ref = rest[:-1], rest[-1]
    k = pl.program_id(1)
    bm, width = acc_ref.shape
    res_w = width // n_res

    @pl.when(k == 0)
    def _():
        acc_ref[...] = jnp.zeros_like(acc_ref)

    @pl.when(k < nk)
    def _():
        for r in range(bm // MM_ROWS):
            rows = slice(r * MM_ROWS, (r + 1) * MM_ROWS)
            for j in range(width // MM_COLS):
                cols = slice(j * MM_COLS, (j + 1) * MM_COLS)
                acc_ref[rows, cols] += jnp.dot(a_ref[rows, :], w_ref[:, cols],
                                               preferred_element_type=F32)

    for j in range(n_res):
        @pl.when(k == j)
        def _(j=j):
            cols = slice(j * res_w, (j + 1) * res_w)
            acc_ref[:, cols] += ALPHA * res_ref[...]

    @pl.when(k >= nk)
    def _():
        _layer_norm_step(lambda j, rows: acc_ref[rows, j * LN_COLS:(j + 1) * LN_COLS],
                         k - nk, width // LN_COLS, g_ref, b_ref, out_refs)


def _mm_acc_res_ln(a, w, layer, res, g, b, out_dtypes, *, bm, bk, name):
    M, K = a.shape
    width = w.shape[2]
    nk = K // bk
    ln_steps = bm // LN_ROWS
    n_res = min(nk, 8)
    res_w = width // n_res
    out_specs, out_shape, out_vmem = _ln_outputs(nk, ln_steps, M, width, out_dtypes)
    est = 2 * (_nbytes((bm, bk), BF16) + _nbytes((bk, width), BF16) + _nbytes((bm, res_w), F32)) \
        + out_vmem + _nbytes((bm, width), F32) + 4 * _nbytes((MM_ROWS, MM_COLS), F32)
    row_blk, col_blk = _ln_prefetch_maps(nk, M // bm)
    return pl.pallas_call(
        functools.partial(_mm_acc_res_ln_body, nk=nk, n_res=n_res),
        grid=(M // bm, nk + ln_steps),
        in_specs=[pl.BlockSpec((bm, bk), lambda m, k: (row_blk(m, k), col_blk(k))),
                  pl.BlockSpec((None, bk, width), lambda m, k: (layer, col_blk(k), 0)),
                  pl.BlockSpec((bm, res_w),
                               lambda m, k: (row_blk(m, k), jnp.minimum(col_blk(k), n_res - 1))),
                  pl.BlockSpec((1, width), lambda m, k: (0, 0)),
                  pl.BlockSpec((1, width), lambda m, k: (0, 0))],
        out_specs=out_specs, out_shape=out_shape,
        scratch_shapes=[pltpu.VMEM((bm, width), F32)],
        compiler_params=_compiler_params(2, est), name=name,
    )(a, w, res, g.reshape(1, -1), b.reshape(1, -1))


def _mm_cols_res_ln_body(a_ref, w_ref, res_ref, g_ref, b_ref, *rest, n_cols):
    out_refs, acc_ref = rest[:-1], rest[-1]
    n = pl.program_id(1)
    bm = acc_ref.shape[1]

    @pl.when(n < n_cols)
    def _():
        for r in range(bm // MM_ROWS):
            rows = slice(r * MM_ROWS, (r + 1) * MM_ROWS)
            acc_ref[n, rows, :] = ALPHA * res_ref[rows, :] + jnp.dot(
                a_ref[rows, :], w_ref[...], preferred_element_type=F32)

    @pl.when(n >= n_cols)
    def _():
        _layer_norm_step(lambda j, rows: acc_ref[j, rows, :], n - n_cols, n_cols,
                         g_ref, b_ref, out_refs)


def _mm_cols_res_ln(a, w, layer, res, g, b, out_dtypes, *, bm, name):
    M, K = a.shape
    width = w.shape[2]
    bn = LN_COLS
    n_cols = width // bn
    ln_steps = bm // LN_ROWS
    out_specs, out_shape, out_vmem = _ln_outputs(n_cols, ln_steps, M, width, out_dtypes)
    est = 2 * (_nbytes((bm, K), BF16) + _nbytes((K, bn), BF16) + _nbytes((bm, bn), F32)) \
        + out_vmem + _nbytes((bm, width), F32) + 2 * _nbytes((MM_ROWS, bn), F32)
    row_blk, col_blk = _ln_prefetch_maps(n_cols, M // bm)
    return pl.pallas_call(
        functools.partial(_mm_cols_res_ln_body, n_cols=n_cols),
        grid=(M // bm, n_cols + ln_steps),
        in_specs=[pl.BlockSpec((bm, K), lambda m, n: (row_blk(m, n), 0)),
                  pl.BlockSpec((None, K, bn), lambda m, n: (layer, 0, col_blk(n))),
                  pl.BlockSpec((bm, bn), lambda m, n: (row_blk(m, n), col_blk(n))),
                  pl.BlockSpec((1, width), lambda m, n: (0, 0)),
                  pl.BlockSpec((1, width), lambda m, n: (0, 0))],
        out_specs=out_specs, out_shape=out_shape,
        scratch_shapes=[pltpu.VMEM((n_cols, bm, bn), F32)],
        compiler_params=_compiler_params(2, est), name=name,
    )(a, w, res, g.reshape(1, -1), b.reshape(1, -1))


CAST_ROWS = 512


def _cast_pad_rows_body(w_ref, o_ref, *, rows_in):
    row = pl.program_id(1) * CAST_ROWS + lax.broadcasted_iota(jnp.int32, w_ref.shape, 0)
    o_ref[...] = jnp.where(row < rows_in, w_ref[...], 0.0).astype(o_ref.dtype)


def _cast_pad_rows(w, rows_out):
    depth, rows_in, cols = w.shape
    spec = pl.BlockSpec((None, CAST_ROWS, cols), lambda l, r: (l, r, 0))
    est = 2 * (_nbytes((CAST_ROWS, cols), F32) + _nbytes((CAST_ROWS, cols), BF16))
    return pl.pallas_call(
        functools.partial(_cast_pad_rows_body, rows_in=rows_in),
        grid=(depth, rows_out // CAST_ROWS),
        in_specs=[spec], out_specs=spec,
        out_shape=jax.ShapeDtypeStruct((depth, rows_out, cols), BF16),
        compiler_params=_compiler_params(2, est), name="cast_pad_rows",
    )(w)


def _layer(l, h32, hbf, seq_len, wts, sinks, conf_dw, conf_dw_b, conf_ln_g, conf_ln_b, sc_dw,
           ln1_g, ln1_b, ln2_g, ln2_b):
    w_in_b, pa_b, pb_b, pc_b, w_out_b, w_ffn_in_b, w_ffn_down_b = wts
    (qkv,) = _proj(hbf, w_in_b, l, (OFF_Q,), _qkv_epilogue, (BF16,),
                   n_cols=QKV_W, out_cols=QKV_W, bm=2048, bn=512, name="qkv_proj")
    c, gb, gx = _proj(hbf, w_in_b, l,
                      (OFF_CONF_VAL, OFF_CONF_GATE, OFF_SC_B, OFF_SC_C, OFF_SC_X),
                      _mixer_epilogue, (F32, F32, F32), n_cols=CONF_W, out_cols=CONF_W,
                      bm=1024, bn=256, name="mixer_proj")
    o_a = _attention(qkv, sinks, seq_len)
    o_b, o_c = _conv_mixers(c, gb, gx, conf_dw, conf_dw_b, conf_ln_g, conf_ln_b, sc_dw, seq_len)
    merged = _merge(hbf, o_a, o_b, o_c, w_in_b, pa_b, pb_b, pc_b, l, bm=1024, bn=256)
    x32, xbf = _mm_cols_res_ln(merged, w_out_b, l, h32, ln1_g, ln1_b, (F32, BF16),
                               bm=1024, name="out_proj_ln")
    (act,) = _proj(xbf, w_ffn_in_b, l, (0, D_FF), _swiglu_epilogue, (BF16,),
                   n_cols=D_FF, out_cols=D_FF_PAD, bm=2048, bn=256, name="ffn_in_swiglu")
    out_dtypes = (F32,) if l == DEPTH - 1 else (F32, BF16)
    outs = _mm_acc_res_ln(act, w_ffn_down_b, l, x32, ln2_g, ln2_b, out_dtypes,
                          bm=1024, bk=1024, name="ffn_down_ln")
    return outs[0], outs[-1]


def kernel(x, w_in, attn_sinks, conf_dw, conf_dw_b, conf_ln_g, conf_ln_b, sc_dw, w_proj_attn,
           w_proj_conf, w_proj_sc, w_out, ln1_g, ln1_b, w_ffn_in, w_ffn_down, ln2_g, ln2_b):
    batch, seq_len, d_model = x.shape
    h32 = x.reshape(batch * seq_len, d_model)
    hbf = h32.astype(BF16)
    wts = (w_in.astype(BF16), w_proj_attn.astype(BF16), w_proj_conf.astype(BF16),
           w_proj_sc.astype(BF16), w_out.astype(BF16), w_ffn_in.astype(BF16),
           _cast_pad_rows(w_ffn_down, D_FF_PAD))
    for l in range(DEPTH):
        h32, hbf = _layer(l, h32, hbf, seq_len, wts, attn_sinks[l], conf_dw[l], conf_dw_b[l],
                          conf_ln_g[l], conf_ln_b[l], sc_dw[l], ln1_g[l], ln1_b[l],
                          ln2_g[l], ln2_b[l])
    return h32.reshape(batch, seq_len, d_model)
```

```python
import functools

import jax
import jax.numpy as jnp
from jax import lax
from jax.experimental import pallas as pl
from jax.experimental.pallas import tpu as pltpu

D_MODEL = 4096
DEPTH = 2
HEAD_DIM = 64
N_Q_HEADS = 32
N_KV_HEADS = 4
GROUP = 8
BLOCK = 128
ATTN_W = N_Q_HEADS * HEAD_DIM
KV_W = N_KV_HEADS * HEAD_DIM
CONF_W = 1024
CONF_K = 31
SC_W = 1024
SC_K = 3
D_FF = 11008
ALPHA = (2 * DEPTH) ** 0.25
LN_EPS = 1e-5
NEG_INF = -1e30

OFF_Q = 0
OFF_K = ATTN_W
OFF_V = ATTN_W + KV_W
OFF_CONF_VAL = ATTN_W + 2 * KV_W
OFF_CONF_GATE = OFF_CONF_VAL + CONF_W
OFF_SC_B = OFF_CONF_GATE + CONF_W
OFF_SC_C = OFF_SC_B + SC_W
OFF_SC_X = OFF_SC_C + SC_W
OFF_GATE = OFF_SC_X + SC_W
QKV_W = ATTN_W + 2 * KV_W

V7X_VMEM_BYTES = 64 * 1024 * 1024
V7X_VMEM_CAP = V7X_VMEM_BYTES - 4 * 1024 * 1024
LANE = 128
SUBLANE = 8

F32 = jnp.float32
BF16 = jnp.bfloat16

QKV_TILE = (2048, 512)
MIXER_TILE = (1024, 256)
MERGE_TILE = (1024, 256)
OUT_PROJ_ROWS = 1024
FFN_IN_TILE = (2048, 256)
FFN_DOWN_TILE = (1024, 1024)
D_FF_PAD = -(-D_FF // FFN_DOWN_TILE[1]) * FFN_DOWN_TILE[1]
DOT_ROWS = 512
DOT_COLS = 512

VMEM_TEMP_FRACTION = 0.2
VMEM_TEMP_BYTES = 8 * 1024 * 1024


def _compiler_params(n_grid, vmem_estimate):
    limit = min(V7X_VMEM_CAP, int(vmem_estimate * (1 + VMEM_TEMP_FRACTION)) + VMEM_TEMP_BYTES)
    return pltpu.CompilerParams(
        dimension_semantics=("arbitrary",) * n_grid, vmem_limit_bytes=limit)


def _nbytes(shape, dtype):
    n = 1
    for s in shape:
        n *= s
    return n * jnp.dtype(dtype).itemsize


def _proj_body(*refs, n_dots, epilogue, n_real, n_total):
    x_ref = refs[0]
    w_refs = refs[1:1 + n_dots]
    o_refs = refs[1 + n_dots:]

    def compute():
        for r in range(x_ref.shape[0] // DOT_ROWS):
            rows = slice(r * DOT_ROWS, (r + 1) * DOT_ROWS)
            x = x_ref[rows, :]
            zs = [jnp.dot(x, w_ref[...], preferred_element_type=F32) for w_ref in w_refs]
            outs = epilogue(*zs)
            for o_ref, o in zip(o_refs, outs):
                o_ref[rows, :] = o.astype(o_ref.dtype)

    if n_real == n_total:
        compute()
        return

    n = pl.program_id(1)
    pl.when(n < n_real)(compute)

    @pl.when(n >= n_real)
    def _():
        for o_ref in o_refs:
            o_ref[...] = jnp.zeros_like(o_ref)


def _col_index(m, n, *, layer, base, last):
    return (layer, 0, base + jnp.minimum(n, last))


def _proj(lhs, w, layer, col_offsets, epilogue, out_dtypes, *, n_cols, out_cols, bm, bn, name):
    M, K = lhs.shape
    n_real = n_cols // bn
    last_m = M // bm - 1
    in_specs = [pl.BlockSpec(
        (bm, K), lambda m, n: (jnp.where(n >= n_real, jnp.minimum(m + 1, last_m), m), 0))]
    for off in col_offsets:
        assert off % bn == 0
        in_specs.append(pl.BlockSpec(
            (None, K, bn),
            functools.partial(_col_index, layer=layer, base=off // bn, last=n_real - 1)))
    out_specs = [pl.BlockSpec((bm, bn), lambda m, n: (m, n)) for _ in out_dtypes]
    out_shape = [jax.ShapeDtypeStruct((M, out_cols), dt) for dt in out_dtypes]
    est = 2 * (_nbytes((bm, K), lhs.dtype) + len(col_offsets) * _nbytes((K, bn), w.dtype)
               + sum(_nbytes((bm, bn), dt) for dt in out_dtypes))
    est += (len(col_offsets) + len(out_dtypes)) * _nbytes((bm, bn), F32)
    return pl.pallas_call(
        functools.partial(_proj_body, n_dots=len(col_offsets), epilogue=epilogue, n_real=n_real,
                          n_total=out_cols // bn),
        grid=(M // bm, out_cols // bn),
        in_specs=in_specs, out_specs=out_specs, out_shape=out_shape,
        compiler_params=_compiler_params(2, est), name=name,
    )(lhs, *([w] * len(col_offsets)))


def _qkv_epilogue(z):
    return (z,)


def _mixer_epilogue(c_val, c_gate, g_b, g_c, x_in):
    return (c_val * jax.nn.sigmoid(c_gate), g_b, g_c * x_in)


def _swiglu_epilogue(f_gate, f_up):
    return (jax.nn.silu(f_gate) * f_up,)


def _attn_body(sink_ref, q_ref, kp_ref, kc_ref, vp_ref, vc_ref, bias_ref, o_ref):
    bias = bias_ref[0]
    lo = lax.broadcasted_iota(jnp.int32, (2 * BLOCK, LANE), 1) < HEAD_DIM
    lo_out = lax.broadcasted_iota(jnp.int32, (4 * BLOCK, LANE), 1) < HEAD_DIM
    first_keys = lax.broadcasted_iota(jnp.int32, (4 * BLOCK, LANE), 0) < 2 * BLOCK
    ones_blk = jnp.where(first_keys == lo_out, 1.0, 0.0).astype(BF16)
    scale = HEAD_DIM ** -0.5
    for j in range(N_KV_HEADS // 2):
        lanes = slice(j * LANE, (j + 1) * LANE)
        kcat = jnp.concatenate([kp_ref[:, lanes], kc_ref[:, lanes]], axis=0).astype(F32) * scale
        vcat = jnp.concatenate([vp_ref[:, lanes], vc_ref[:, lanes]], axis=0).astype(F32)
        krot = pltpu.roll(kcat, HEAD_DIM, axis=1)
        vrot = pltpu.roll(vcat, HEAD_DIM, axis=1)
        for hh in range(2):
            h = 2 * j + hh
            k_lo, k_hi = (kcat, krot) if hh == 0 else (krot, kcat)
            v_lo, v_hi = (vcat, vrot) if hh == 0 else (vrot, vcat)
            kk = jnp.concatenate([jnp.where(lo, k_lo, 0.0), jnp.where(lo, 0.0, k_hi)],
                                 axis=0).astype(BF16)
            vv = jnp.concatenate([jnp.where(lo, v_lo, 0.0), jnp.where(lo, 0.0, v_hi)],
                                 axis=0).astype(BF16)
            rhs = jnp.concatenate([vv, ones_blk], axis=1)
            base = h * GROUP * HEAD_DIM
            qs = jnp.concatenate(
                [q_ref[:, base + p * LANE: base + (p + 1) * LANE] for p in range(GROUP // 2)],
                axis=0)
            s = lax.dot_general(qs, kk, (((1,), (1,)), ((), ())),
                                preferred_element_type=F32) + bias
            sink_e = [jnp.concatenate(
                [jnp.full((BLOCK, LANE), sink_ref[h * GROUP + 2 * p + e], F32)
                 for p in range(GROUP // 2)], axis=0) for e in range(2)]
            ps, ms = [], []
            for e in range(2):
                se = s[:, e * 2 * BLOCK:(e + 1) * 2 * BLOCK]
                row_max = jnp.max(se, axis=-1, keepdims=True)
                m = jnp.maximum(jnp.broadcast_to(row_max, (4 * BLOCK, LANE)), sink_e[e])
                ms.append(m)
                ps.append(jnp.exp(se - jnp.concatenate([m, m], axis=1)).astype(BF16))
            pcat = jnp.concatenate(ps, axis=1)
            ov = jnp.dot(pcat, rhs, preferred_element_type=F32)
            m_sel = jnp.where(lo_out, ms[0], ms[1])
            sink_sel = jnp.where(lo_out, sink_e[0], sink_e[1])
            den = ov[:, LANE:] + jnp.exp(sink_sel - m_sel)
            o = ov[:, :LANE] / den
            for p in range(GROUP // 2):
                o_ref[:, base + p * LANE: base + (p + 1) * LANE] = (
                    o[p * BLOCK:(p + 1) * BLOCK].astype(o_ref.dtype))


def _attn_bias():
    r = jnp.arange(4 * BLOCK)[:, None] % BLOCK
    c = jnp.arange(4 * BLOCK)[None, :] % (2 * BLOCK)
    band = (c - r >= 1) & (c - r <= BLOCK)
    first = band & (c >= BLOCK)
    return jnp.where(jnp.stack([band, first]), 0.0, NEG_INF).astype(F32)


def _attention(qkv, sinks, seq_len):
    M = qkv.shape[0]
    blocks_per_seq = seq_len // BLOCK
    k_col = OFF_K // KV_W
    v_col = OFF_V // KV_W

    def prev(i):
        return jnp.maximum(i - 1, 0)

    in_specs = [
        pl.BlockSpec(memory_space=pltpu.SMEM),
        pl.BlockSpec((BLOCK, ATTN_W), lambda i: (i, 0)),
        pl.BlockSpec((BLOCK, KV_W), lambda i: (prev(i), k_col)),
        pl.BlockSpec((BLOCK, KV_W), lambda i: (i, k_col)),
        pl.BlockSpec((BLOCK, KV_W), lambda i: (prev(i), v_col)),
        pl.BlockSpec((BLOCK, KV_W), lambda i: (i, v_col)),
        pl.BlockSpec((1, 4 * BLOCK, 4 * BLOCK),
                     lambda i: (jnp.where(i % blocks_per_seq == 0, 1, 0), 0, 0)),
    ]
    est = 2 * (2 * _nbytes((BLOCK, ATTN_W), BF16) + 4 * _nbytes((BLOCK, KV_W), BF16)
               + _nbytes((4 * BLOCK, 4 * BLOCK), F32)) + 16 * _nbytes((4 * BLOCK, 4 * BLOCK), F32)
    return pl.pallas_call(
        _attn_body,
        grid=(M // BLOCK,),
        in_specs=in_specs,
        out_specs=pl.BlockSpec((BLOCK, ATTN_W), lambda i: (i, 0)),
        out_shape=jax.ShapeDtypeStruct((M, ATTN_W), BF16),
        compiler_params=_compiler_params(1, est), name="swa_attention",
    )(sinks, qkv, qkv, qkv, qkv, qkv, _attn_bias())


CONV_BT = 256
CONF_HALO = 32
SC_HALO = 8
CONV_ROWS = 64
CONV_LANES = 256
CONV_SHIFT_ROWS = CONV_BT + CONF_HALO - SUBLANE


def _conv_body(c_ref, ch_ref, gb_ref, gx_ref, gxh_ref, cw_ref, cb_ref, lg_ref, lb_ref, sw_ref,
               ob_ref, oc_ref, cext_ref, y_ref, gext_ref, shift_ref, *, blocks_per_seq):
    first = (pl.program_id(0) % blocks_per_seq) == 0
    cext_ref[0:CONF_HALO, :] = jnp.where(first, 0.0, ch_ref[...])
    cext_ref[CONF_HALO:, :] = c_ref[...]
    gext_ref[0:SC_HALO, :] = jnp.where(first, 0.0, gxh_ref[...])
    gext_ref[SC_HALO:, :] = gx_ref[...]
    for s in range(1, SUBLANE):
        shift_ref[s - 1] = cext_ref[s:s + CONV_SHIFT_ROWS, :]

    for lc in range(CONF_W // CONV_LANES):
        lanes = slice(lc * CONV_LANES, (lc + 1) * CONV_LANES)
        for rb in range(CONV_BT // CONV_ROWS):
            r0 = rb * CONV_ROWS
            acc = jnp.zeros((CONV_ROWS, CONV_LANES), F32)
            for k in range(CONF_K):
                off = CONF_HALO - (CONF_K - 1) + k
                s, start = off % SUBLANE, r0 + off - off % SUBLANE
                if s == 0:
                    tap = cext_ref[start:start + CONV_ROWS, lanes]
                else:
                    tap = shift_ref[s - 1, start:start + CONV_ROWS, lanes]
                acc = acc + cw_ref[k:k + 1, lanes] * tap
            y_ref[r0:r0 + CONV_ROWS, lanes] = acc + cb_ref[:, lanes]
            acc2 = jnp.zeros((CONV_ROWS, CONV_LANES), F32)
            for k in range(SC_K):
                start = r0 + SC_HALO - (SC_K - 1) + k
                acc2 = acc2 + sw_ref[k:k + 1, lanes] * gext_ref[start:start + CONV_ROWS, lanes]
            oc_ref[r0:r0 + CONV_ROWS, lanes] = (
                gb_ref[r0:r0 + CONV_ROWS, lanes] * acc2).astype(oc_ref.dtype)

    y = y_ref[...]
    mu = jnp.mean(y, axis=-1, keepdims=True)
    yc = y - mu
    var = jnp.mean(yc * yc, axis=-1, keepdims=True)
    yn = yc * lax.rsqrt(var + LN_EPS) * lg_ref[...] + lb_ref[...]
    ob_ref[...] = (yn * jax.nn.sigmoid(yn)).astype(ob_ref.dtype)


def _conv_mixers(c, gb, gx, conf_dw, conf_dw_b, conf_ln_g, conf_ln_b, sc_dw, seq_len):
    M = c.shape[0]
    bt = CONV_BT
    row = lambda i: (i, 0)
    whole = lambda i: (0, 0)
    in_specs = [
        pl.BlockSpec((bt, CONF_W), row),
        pl.BlockSpec((CONF_HALO, CONF_W), lambda i: (jnp.maximum(i * (bt // CONF_HALO) - 1, 0), 0)),
        pl.BlockSpec((bt, SC_W), row),
        pl.BlockSpec((bt, SC_W), row),
        pl.BlockSpec((SC_HALO, SC_W), lambda i: (jnp.maximum(i * (bt // SC_HALO) - 1, 0), 0)),
        pl.BlockSpec((CONF_K, CONF_W), whole),
        pl.BlockSpec((1, CONF_W), whole),
        pl.BlockSpec((1, CONF_W), whole),
        pl.BlockSpec((1, CONF_W), whole),
        pl.BlockSpec((SC_K, SC_W), whole),
    ]
    est = 2 * (3 * _nbytes((bt, CONF_W), F32) + 2 * _nbytes((bt, CONF_W), BF16)) \
        + 6 * _nbytes((bt + CONF_HALO, CONF_W), F32) \
        + _nbytes((SUBLANE - 1, CONV_SHIFT_ROWS, CONF_W), F32)
    return pl.pallas_call(
        functools.partial(_conv_body, blocks_per_seq=seq_len // bt),
        grid=(M // bt,),
        in_specs=in_specs,
        out_specs=[pl.BlockSpec((bt, CONF_W), row), pl.BlockSpec((bt, SC_W), row)],
        out_shape=[jax.ShapeDtypeStruct((M, CONF_W), BF16), jax.ShapeDtypeStruct((M, SC_W), BF16)],
        scratch_shapes=[pltpu.VMEM((bt + CONF_HALO, CONF_W), F32),
                        pltpu.VMEM((bt, CONF_W), F32),
                        pltpu.VMEM((bt + SC_HALO, SC_W), F32),
                        pltpu.VMEM((SUBLANE - 1, CONV_SHIFT_ROWS, CONF_W), F32)],
        compiler_params=_compiler_params(1, est), name="conv_mixers",
    )(c, c, gb, gx, gx, conf_dw, conf_dw_b.reshape(1, -1), conf_ln_g.reshape(1, -1),
      conf_ln_b.reshape(1, -1), sc_dw)


def _merge_body(x_ref, oa_ref, ob_ref, oc_ref, wg0_ref, wg1_ref, wg2_ref,
                pa_ref, pb_ref, pc_ref, out_ref):
    for r in range(out_ref.shape[0] // DOT_ROWS):
        rows = slice(r * DOT_ROWS, (r + 1) * DOT_ROWS)
        acc = None
        for o_ref, wg_ref, p_ref in ((oa_ref, wg0_ref, pa_ref), (ob_ref, wg1_ref, pb_ref),
                                     (oc_ref, wg2_ref, pc_ref)):
            gate = jax.nn.sigmoid(
                jnp.dot(x_ref[rows, :], wg_ref[...], preferred_element_type=F32))
            term = gate * jnp.dot(o_ref[rows, :], p_ref[...], preferred_element_type=F32)
            acc = term if acc is None else acc + term
        out_ref[rows, :] = acc.astype(out_ref.dtype)


def _merge(xb, oa, ob, oc, w_in_b, pa, pb, pc, layer, *, bm, bn):
    M = xb.shape[0]
    n_blocks = D_MODEL // bn
    lhs_spec = lambda a: pl.BlockSpec((bm, a.shape[1]), lambda m, n: (m, 0))
    gate_spec = lambda i: pl.BlockSpec(
        (None, D_MODEL, bn),
        functools.partial(_col_index, layer=layer, base=(OFF_GATE + i * D_MODEL) // bn,
                          last=n_blocks - 1))
    p_spec = lambda p: pl.BlockSpec((None, p.shape[1], bn), lambda m, n: (layer, 0, n))
    k_total = D_MODEL + ATTN_W + CONF_W + SC_W
    est = 2 * (_nbytes((bm, k_total), BF16) + _nbytes((3 * D_MODEL + k_total - D_MODEL, bn), BF16)
               + _nbytes((bm, bn), BF16)) + 8 * _nbytes((bm, bn), F32)
    return pl.pallas_call(
        _merge_body,
        grid=(M // bm, D_MODEL // bn),
        in_specs=[lhs_spec(xb), lhs_spec(oa), lhs_spec(ob), lhs_spec(oc),
                  gate_spec(0), gate_spec(1), gate_spec(2), p_spec(pa), p_spec(pb), p_spec(pc)],
        out_specs=pl.BlockSpec((bm, bn), lambda m, n: (m, n)),
        out_shape=jax.ShapeDtypeStruct((M, D_MODEL), BF16),
        compiler_params=_compiler_params(2, est), name="gated_merge",
    )(xb, oa, ob, oc, w_in_b, w_in_b, w_in_b, pa, pb, pc)


LN_ROWS = 256
LN_CHUNK = 128
LN_COLS = DOT_COLS


def _layer_norm_step(load_piece, step, n_pieces, g_ref, b_ref, out_refs):
    width = n_pieces * LN_COLS
    for i in range(LN_ROWS // LN_CHUNK):
        rows = pl.ds(pl.multiple_of(step * LN_ROWS + i * LN_CHUNK, LN_CHUNK), LN_CHUNK)
        out_rows = slice(i * LN_CHUNK, (i + 1) * LN_CHUNK)
        total = None
        for j in range(n_pieces):
            s = jnp.sum(load_piece(j, rows), axis=-1, keepdims=True)
            total = s if total is None else total + s
        mu = total / width
        total = None
        for j in range(n_pieces):
            d = load_piece(j, rows) - mu
            s = jnp.sum(d * d, axis=-1, keepdims=True)
            total = s if total is None else total + s
        rstd = lax.rsqrt(total / width + LN_EPS)
        for j in range(n_pieces):
            cols = slice(j * LN_COLS, (j + 1) * LN_COLS)
            out = (load_piece(j, rows) - mu) * rstd * g_ref[:, cols] + b_ref[:, cols]
            for o_ref in out_refs:
                o_ref[out_rows, cols] = out.astype(o_ref.dtype)


def _ln_prefetch_maps(n_main, n_row_blocks):
    def row_blk(m, s):
        return jnp.where(s >= n_main, jnp.minimum(m + 1, n_row_blocks - 1), m)

    def col_blk(s):
        return jnp.where(s >= n_main, 0, s)

    return row_blk, col_blk


def _ln_outputs(n_main, ln_steps, n_rows, width, out_dtypes):
    index = lambda m, s: (m * ln_steps + jnp.maximum(s - n_main, 0), 0)
    specs = [pl.BlockSpec((LN_ROWS, width), index) for _ in out_dtypes]
    shapes = [jax.ShapeDtypeStruct((n_rows, width), dt) for dt in out_dtypes]
    vmem = 2 * sum(_nbytes((LN_ROWS, width), dt) for dt in out_dtypes)
    return specs, shapes, vmem


def _mm_acc_res_ln_body(a_ref, w_ref, res_ref, g_ref, b_ref, *rest, nk, n_res):
    out_refs, acc_ref = rest[:-1], rest[-1]
    k = pl.program_id(1)
    bm, width = acc_ref.shape
    res_w = width // n_res

    @pl.when(k == 0)
    def _():
        acc_ref[...] = jnp.zeros_like(acc_ref)

    @pl.when(k < nk)
    def _():
        for r in range(bm // DOT_ROWS):
            rows = slice(r * DOT_ROWS, (r + 1) * DOT_ROWS)
            for j in range(width // DOT_COLS):
                cols = slice(j * DOT_COLS, (j + 1) * DOT_COLS)
                acc_ref[rows, cols] += jnp.dot(a_ref[rows, :], w_ref[:, cols],
                                               preferred_element_type=F32)

    for j in range(n_res):
        @pl.when(k == j)
        def _(j=j):
            cols = slice(j * res_w, (j + 1) * res_w)
            acc_ref[:, cols] += ALPHA * res_ref[...]

    @pl.when(k >= nk)
    def _():
        _layer_norm_step(lambda j, rows: acc_ref[rows, j * LN_COLS:(j + 1) * LN_COLS],
                         k - nk, width // LN_COLS, g_ref, b_ref, out_refs)


def _mm_acc_res_ln(a, w, layer, res, g, b, out_dtypes, *, bm, bk, name):
    M, K = a.shape
    width = w.shape[2]
    nk = K // bk
    ln_steps = bm // LN_ROWS
    n_res = min(nk, 8)
    res_w = width // n_res
    out_specs, out_shape, out_vmem = _ln_outputs(nk, ln_steps, M, width, out_dtypes)
    est = 2 * (_nbytes((bm, bk), BF16) + _nbytes((bk, width), BF16) + _nbytes((bm, res_w), F32)) \
        + out_vmem + _nbytes((bm, width), F32) + 4 * _nbytes((DOT_ROWS, DOT_COLS), F32)
    row_blk, col_blk = _ln_prefetch_maps(nk, M // bm)
    return pl.pallas_call(
        functools.partial(_mm_acc_res_ln_body, nk=nk, n_res=n_res),
        grid=(M // bm, nk + ln_steps),
        in_specs=[pl.BlockSpec((bm, bk), lambda m, k: (row_blk(m, k), col_blk(k))),
                  pl.BlockSpec((None, bk, width), lambda m, k: (layer, col_blk(k), 0)),
                  pl.BlockSpec((bm, res_w),
                               lambda m, k: (row_blk(m, k), jnp.minimum(col_blk(k), n_res - 1))),
                  pl.BlockSpec((1, width), lambda m, k: (0, 0)),
                  pl.BlockSpec((1, width), lambda m, k: (0, 0))],
        out_specs=out_specs, out_shape=out_shape,
        scratch_shapes=[pltpu.VMEM((bm, width), F32)],
        compiler_params=_compiler_params(2, est), name=name,
    )(a, w, res, g.reshape(1, -1), b.reshape(1, -1))


def _mm_cols_res_ln_body(a_ref, w_ref, res_ref, g_ref, b_ref, *rest, n_cols):
    out_refs, acc_ref = rest[:-1], rest[-1]
    n = pl.program_id(1)
    bm = acc_ref.shape[1]

    @pl.when(n < n_cols)
    def _():
        for r in range(bm // DOT_ROWS):
            rows = slice(r * DOT_ROWS, (r + 1) * DOT_ROWS)
            acc_ref[n, rows, :] = ALPHA * res_ref[rows, :] + jnp.dot(
                a_ref[rows, :], w_ref[...], preferred_element_type=F32)

    @pl.when(n >= n_cols)
    def _():
        _layer_norm_step(lambda j, rows: acc_ref[j, rows, :], n - n_cols, n_cols,
                         g_ref, b_ref, out_refs)


def _mm_cols_res_ln(a, w, layer, res, g, b, out_dtypes, *, bm, name):
    M, K = a.shape
    width = w.shape[2]
    bn = LN_COLS
    n_cols = width // bn
    ln_steps = bm // LN_ROWS
    out_specs, out_shape, out_vmem = _ln_outputs(n_cols, ln_steps, M, width, out_dtypes)
    est = 2 * (_nbytes((bm, K), BF16) + _nbytes((K, bn), BF16) + _nbytes((bm, bn), F32)) \
        + out_vmem + _nbytes((bm, width), F32) + 2 * _nbytes((DOT_ROWS, bn), F32)
    row_blk, col_blk = _ln_prefetch_maps(n_cols, M // bm)
    return pl.pallas_call(
        functools.partial(_mm_cols_res_ln_body, n_cols=n_cols),
        grid=(M // bm, n_cols + ln_steps),
        in_specs=[pl.BlockSpec((bm, K), lambda m, n: (row_blk(m, n), 0)),
                  pl.BlockSpec((None, K, bn), lambda m, n: (layer, 0, col_blk(n))),
                  pl.BlockSpec((bm, bn), lambda m, n: (row_blk(m, n), col_blk(n))),
                  pl.BlockSpec((1, width), lambda m, n: (0, 0)),
                  pl.BlockSpec((1, width), lambda m, n: (0, 0))],
        out_specs=out_specs, out_shape=out_shape,
        scratch_shapes=[pltpu.VMEM((n_cols, bm, bn), F32)],
        compiler_params=_compiler_params(2, est), name=name,
    )(a, w, res, g.reshape(1, -1), b.reshape(1, -1))


CAST_ROWS = 512


def _cast_pad_rows_body(w_ref, o_ref, *, rows_in):
    row = pl.program_id(1) * CAST_ROWS + lax.broadcasted_iota(jnp.int32, w_ref.shape, 0)
    o_ref[...] = jnp.where(row < rows_in, w_ref[...], 0.0).astype(o_ref.dtype)


def _cast_pad_rows(w, rows_out):
    depth, rows_in, cols = w.shape
    spec = pl.BlockSpec((None, CAST_ROWS, cols), lambda l, r: (l, r, 0))
    est = 2 * (_nbytes((CAST_ROWS, cols), F32) + _nbytes((CAST_ROWS, cols), BF16))
    return pl.pallas_call(
        functools.partial(_cast_pad_rows_body, rows_in=rows_in),
        grid=(depth, rows_out // CAST_ROWS),
        in_specs=[spec], out_specs=spec,
        out_shape=jax.ShapeDtypeStruct((depth, rows_out, cols), BF16),
        compiler_params=_compiler_params(2, est), name="cast_pad_rows",
    )(w)


def _layer(l, h32, hbf, seq_len, wts, sinks, conf_dw, conf_dw_b, conf_ln_g, conf_ln_b, sc_dw,
           ln1_g, ln1_b, ln2_g, ln2_b):
    w_in_b, pa_b, pb_b, pc_b, w_out_b, w_ffn_in_b, w_ffn_down_b = wts
    (qkv,) = _proj(hbf, w_in_b, l, (OFF_Q,), _qkv_epilogue, (BF16,), n_cols=QKV_W,
                   out_cols=QKV_W, bm=QKV_TILE[0], bn=QKV_TILE[1], name="qkv_proj")
    c, gb, gx = _proj(hbf, w_in_b, l,
                      (OFF_CONF_VAL, OFF_CONF_GATE, OFF_SC_B, OFF_SC_C, OFF_SC_X),
                      _mixer_epilogue, (F32, F32, F32), n_cols=CONF_W, out_cols=CONF_W,
                      bm=MIXER_TILE[0], bn=MIXER_TILE[1], name="mixer_proj")
    o_a = _attention(qkv, sinks, seq_len)
    o_b, o_c = _conv_mixers(c, gb, gx, conf_dw, conf_dw_b, conf_ln_g, conf_ln_b, sc_dw, seq_len)
    merged = _merge(hbf, o_a, o_b, o_c, w_in_b, pa_b, pb_b, pc_b, l,
                    bm=MERGE_TILE[0], bn=MERGE_TILE[1])
    x32, xbf = _mm_cols_res_ln(merged, w_out_b, l, h32, ln1_g, ln1_b, (F32, BF16),
                               bm=OUT_PROJ_ROWS, name="out_proj_ln")
    (act,) = _proj(xbf, w_ffn_in_b, l, (0, D_FF), _swiglu_epilogue, (BF16,), n_cols=D_FF,
                   out_cols=D_FF_PAD, bm=FFN_IN_TILE[0], bn=FFN_IN_TILE[1],
                   name="ffn_in_swiglu")
    out_dtypes = (F32,) if l == DEPTH - 1 else (F32, BF16)
    outs = _mm_acc_res_ln(act, w_ffn_down_b, l, x32, ln2_g, ln2_b, out_dtypes,
                          bm=FFN_DOWN_TILE[0], bk=FFN_DOWN_TILE[1], name="ffn_down_ln")
    return outs[0], outs[-1]


def kernel(x, w_in, attn_sinks, conf_dw, conf_dw_b, conf_ln_g, conf_ln_b, sc_dw, w_proj_attn,
           w_proj_conf, w_proj_sc, w_out, ln1_g, ln1_b, w_ffn_in, w_ffn_down, ln2_g, ln2_b):
    batch, seq_len, d_model = x.shape
    h32 = x.reshape(batch * seq_len, d_model)
    hbf = h32.astype(BF16)
    wts = (w_in.astype(BF16), w_proj_attn.astype(BF16), w_proj_conf.astype(BF16),
           w_proj_sc.astype(BF16), w_out.astype(BF16), w_ffn_in.astype(BF16),
           _cast_pad_rows(w_ffn_down, D_FF_PAD))
    for l in range(DEPTH):
        h32, hbf = _layer(l, h32, hbf, seq_len, wts, attn_sinks[l], conf_dw[l], conf_dw_b[l],
                          conf_ln_g[l], conf_ln_b[l], sc_dw[l], ln1_g[l], ln1_b[l],
                          ln2_g[l], ln2_b[l])
    return h32.reshape(batch, seq_len, d_model)
```

```python
import functools

import jax
import jax.numpy as jnp
from jax import lax
from jax.experimental import pallas as pl
from jax.experimental.pallas import tpu as pltpu

D_MODEL = 4096
DEPTH = 2
HEAD_DIM = 64
N_Q_HEADS = 32
N_KV_HEADS = 4
GROUP = 8
BLOCK = 128
ATTN_W = N_Q_HEADS * HEAD_DIM
KV_W = N_KV_HEADS * HEAD_DIM
CONF_W = 1024
CONF_K = 31
SC_W = 1024
SC_K = 3
D_FF = 11008
ALPHA = (2 * DEPTH) ** 0.25
LN_EPS = 1e-5
NEG_INF = -1e30

OFF_Q = 0
OFF_K = ATTN_W
OFF_V = ATTN_W + KV_W
OFF_CONF_VAL = ATTN_W + 2 * KV_W
OFF_CONF_GATE = OFF_CONF_VAL + CONF_W
OFF_SC_B = OFF_CONF_GATE + CONF_W
OFF_SC_C = OFF_SC_B + SC_W
OFF_SC_X = OFF_SC_C + SC_W
OFF_GATE = OFF_SC_X + SC_W
QKV_W = ATTN_W + 2 * KV_W

V7X_VMEM_BYTES = 64 * 1024 * 1024
V7X_VMEM_CAP = V7X_VMEM_BYTES - 4 * 1024 * 1024
LANE = 128
SUBLANE = 8

F32 = jnp.float32
BF16 = jnp.bfloat16

QKV_TILE = (2048, 512)
MIXER_TILE = (1024, 256)
MERGE_TILE = (1024, 256)
OUT_PROJ_ROWS = 1024
FFN_IN_TILE = (2048, 256)
FFN_DOWN_TILE = (1024, 1024)
D_FF_PAD = -(-D_FF // FFN_DOWN_TILE[1]) * FFN_DOWN_TILE[1]
DOT_ROWS = 512
DOT_COLS = 512

VMEM_TEMP_FRACTION = 0.2
VMEM_TEMP_BYTES = 8 * 1024 * 1024


def _compiler_params(n_grid, vmem_estimate):
    limit = min(V7X_VMEM_CAP, int(vmem_estimate * (1 + VMEM_TEMP_FRACTION)) + VMEM_TEMP_BYTES)
    return pltpu.CompilerParams(
        dimension_semantics=("arbitrary",) * n_grid, vmem_limit_bytes=limit)


def _nbytes(shape, dtype):
    n = 1
    for s in shape:
        n *= s
    return n * jnp.dtype(dtype).itemsize


def _proj_body(*refs, n_dots, epilogue, n_real, n_total):
    x_ref = refs[0]
    w_refs = refs[1:1 + n_dots]
    o_refs = refs[1 + n_dots:]

    def compute():
        for r in range(x_ref.shape[0] // DOT_ROWS):
            rows = slice(r * DOT_ROWS, (r + 1) * DOT_ROWS)
            x = x_ref[rows, :]
            zs = [jnp.dot(x, w_ref[...], preferred_element_type=F32) for w_ref in w_refs]
            outs = epilogue(*zs)
            for o_ref, o in zip(o_refs, outs):
                o_ref[rows, :] = o.astype(o_ref.dtype)

    if n_real == n_total:
        compute()
        return

    n = pl.program_id(1)
    pl.when(n < n_real)(compute)

    @pl.when(n >= n_real)
    def _():
        for o_ref in o_refs:
            o_ref[...] = jnp.zeros_like(o_ref)


def _col_index(m, n, *, layer, base, last):
    return (layer, 0, base + jnp.minimum(n, last))


def _proj(lhs, w, layer, col_offsets, epilogue, out_dtypes, *, n_cols, out_cols, bm, bn, name):
    M, K = lhs.shape
    n_real = n_cols // bn
    last_m = M // bm - 1
    in_specs = [pl.BlockSpec(
        (bm, K), lambda m, n: (jnp.where(n >= n_real, jnp.minimum(m + 1, last_m), m), 0))]
    for off in col_offsets:
        assert off % bn == 0
        in_specs.append(pl.BlockSpec(
            (None, K, bn),
            functools.partial(_col_index, layer=layer, base=off // bn, last=n_real - 1)))
    out_specs = [pl.BlockSpec((bm, bn), lambda m, n: (m, n)) for _ in out_dtypes]
    out_shape = [jax.ShapeDtypeStruct((M, out_cols), dt) for dt in out_dtypes]
    est = 2 * (_nbytes((bm, K), lhs.dtype) + len(col_offsets) * _nbytes((K, bn), w.dtype)
               + sum(_nbytes((bm, bn), dt) for dt in out_dtypes))
    est += (len(col_offsets) + len(out_dtypes)) * _nbytes((bm, bn), F32)
    return pl.pallas_call(
        functools.partial(_proj_body, n_dots=len(col_offsets), epilogue=epilogue, n_real=n_real,
                          n_total=out_cols // bn),
        grid=(M // bm, out_cols // bn),
        in_specs=in_specs, out_specs=out_specs, out_shape=out_shape,
        compiler_params=_compiler_params(2, est), name=name,
    )(lhs, *([w] * len(col_offsets)))


def _qkv_epilogue(z):
    return (z,)


def _mixer_epilogue(c_val, c_gate, g_b, g_c, x_in):
    return (c_val * jax.nn.sigmoid(c_gate), g_b, g_c * x_in)


def _swiglu_epilogue(f_gate, f_up):
    return (jax.nn.silu(f_gate) * f_up,)


def _attn_block(sink_ref, q_ref, o_ref, rows, load_k, load_v, bias):
    lo = lax.broadcasted_iota(jnp.int32, (2 * BLOCK, LANE), 1) < HEAD_DIM
    lo_out = lax.broadcasted_iota(jnp.int32, (4 * BLOCK, LANE), 1) < HEAD_DIM
    first_keys = lax.broadcasted_iota(jnp.int32, (4 * BLOCK, LANE), 0) < 2 * BLOCK
    ones_blk = jnp.where(first_keys == lo_out, 1.0, 0.0).astype(BF16)
    scale = HEAD_DIM ** -0.5
    for j in range(N_KV_HEADS // 2):
        lanes = slice(j * LANE, (j + 1) * LANE)
        kcat = load_k(lanes).astype(F32) * scale
        vcat = load_v(lanes).astype(F32)
        krot = pltpu.roll(kcat, HEAD_DIM, axis=1)
        vrot = pltpu.roll(vcat, HEAD_DIM, axis=1)
        for hh in range(2):
            h = 2 * j + hh
            k_lo, k_hi = (kcat, krot) if hh == 0 else (krot, kcat)
            v_lo, v_hi = (vcat, vrot) if hh == 0 else (vrot, vcat)
            kk = jnp.concatenate([jnp.where(lo, k_lo, 0.0), jnp.where(lo, 0.0, k_hi)],
                                 axis=0).astype(BF16)
            vv = jnp.concatenate([jnp.where(lo, v_lo, 0.0), jnp.where(lo, 0.0, v_hi)],
                                 axis=0).astype(BF16)
            rhs = jnp.concatenate([vv, ones_blk], axis=1)
            base = h * GROUP * HEAD_DIM
            qs = jnp.concatenate(
                [q_ref[rows, base + p * LANE: base + (p + 1) * LANE] for p in range(GROUP // 2)],
                axis=0)
            s = lax.dot_general(qs, kk, (((1,), (1,)), ((), ())),
                                preferred_element_type=F32) + bias
            sink_e = [jnp.concatenate(
                [jnp.full((BLOCK, LANE), sink_ref[h * GROUP + 2 * p + e], F32)
                 for p in range(GROUP // 2)], axis=0) for e in range(2)]
            ps, ms = [], []
            for e in range(2):
                se = s[:, e * 2 * BLOCK:(e + 1) * 2 * BLOCK]
                row_max = jnp.max(se, axis=-1, keepdims=True)
                m = jnp.maximum(jnp.broadcast_to(row_max, (4 * BLOCK, LANE)), sink_e[e])
                ms.append(m)
                ps.append(jnp.exp(se - jnp.concatenate([m, m], axis=1)).astype(BF16))
            pcat = jnp.concatenate(ps, axis=1)
            ov = jnp.dot(pcat, rhs, preferred_element_type=F32)
            m_sel = jnp.where(lo_out, ms[0], ms[1])
            sink_sel = jnp.where(lo_out, sink_e[0], sink_e[1])
            den = ov[:, LANE:] + jnp.exp(sink_sel - m_sel)
            o = ov[:, :LANE] / den
            for p in range(GROUP // 2):
                o_ref[rows, base + p * LANE: base + (p + 1) * LANE] = (
                    o[p * BLOCK:(p + 1) * BLOCK].astype(o_ref.dtype))


def _token_mixers_body(sink_ref, q_ref, kp_ref, kc_ref, vp_ref, vc_ref, bias0_ref, bias1_ref,
                       c_ref, ch_ref, gb_ref, gx_ref, gxh_ref, cw_ref, cb_ref, lg_ref, lb_ref,
                       sw_ref, oa_ref, ob_ref, oc_ref, cext_ref, y_ref, gext_ref, shift_ref,
                       *, blocks_per_seq):
    first, second = slice(0, BLOCK), slice(BLOCK, 2 * BLOCK)
    _attn_block(
        sink_ref, q_ref, oa_ref, first,
        lambda lanes: jnp.concatenate([kp_ref[:, lanes], kc_ref[first, lanes]], axis=0),
        lambda lanes: jnp.concatenate([vp_ref[:, lanes], vc_ref[first, lanes]], axis=0),
        bias0_ref[0])
    _conv_body(c_ref, ch_ref, gb_ref, gx_ref, gxh_ref, cw_ref, cb_ref, lg_ref, lb_ref, sw_ref,
               ob_ref, oc_ref, cext_ref, y_ref, gext_ref, shift_ref,
               blocks_per_seq=blocks_per_seq)
    _attn_block(sink_ref, q_ref, oa_ref, second,
                lambda lanes: kc_ref[:, lanes], lambda lanes: vc_ref[:, lanes], bias1_ref[0])


def _attn_bias():
    r = jnp.arange(4 * BLOCK)[:, None] % BLOCK
    c = jnp.arange(4 * BLOCK)[None, :] % (2 * BLOCK)
    band = (c - r >= 1) & (c - r <= BLOCK)
    first = band & (c >= BLOCK)
    return jnp.where(jnp.stack([band, first]), 0.0, NEG_INF).astype(F32)


CONV_BT = 256
CONF_HALO = 32
SC_HALO = 8
CONV_ROWS = 64
CONV_LANES = 256
CONV_SHIFT_ROWS = CONV_BT + CONF_HALO - SUBLANE


def _conv_body(c_ref, ch_ref, gb_ref, gx_ref, gxh_ref, cw_ref, cb_ref, lg_ref, lb_ref, sw_ref,
               ob_ref, oc_ref, cext_ref, y_ref, gext_ref, shift_ref, *, blocks_per_seq):
    first = (pl.program_id(0) % blocks_per_seq) == 0
    cext_ref[0:CONF_HALO, :] = jnp.where(first, 0.0, ch_ref[...])
    cext_ref[CONF_HALO:, :] = c_ref[...]
    gext_ref[0:SC_HALO, :] = jnp.where(first, 0.0, gxh_ref[...])
    gext_ref[SC_HALO:, :] = gx_ref[...]
    for s in range(1, SUBLANE):
        shift_ref[s - 1] = cext_ref[s:s + CONV_SHIFT_ROWS, :]

    for lc in range(CONF_W // CONV_LANES):
        lanes = slice(lc * CONV_LANES, (lc + 1) * CONV_LANES)
        for rb in range(CONV_BT // CONV_ROWS):
            r0 = rb * CONV_ROWS
            acc = jnp.zeros((CONV_ROWS, CONV_LANES), F32)
            for k in range(CONF_K):
                off = CONF_HALO - (CONF_K - 1) + k
                s, start = off % SUBLANE, r0 + off - off % SUBLANE
                if s == 0:
                    tap = cext_ref[start:start + CONV_ROWS, lanes]
                else:
                    tap = shift_ref[s - 1, start:start + CONV_ROWS, lanes]
                acc = acc + cw_ref[k:k + 1, lanes] * tap
            y_ref[r0:r0 + CONV_ROWS, lanes] = acc + cb_ref[:, lanes]
            acc2 = jnp.zeros((CONV_ROWS, CONV_LANES), F32)
            for k in range(SC_K):
                start = r0 + SC_HALO - (SC_K - 1) + k
                acc2 = acc2 + sw_ref[k:k + 1, lanes] * gext_ref[start:start + CONV_ROWS, lanes]
            oc_ref[r0:r0 + CONV_ROWS, lanes] = (
                gb_ref[r0:r0 + CONV_ROWS, lanes] * acc2).astype(oc_ref.dtype)

    y = y_ref[...]
    mu = jnp.mean(y, axis=-1, keepdims=True)
    yc = y - mu
    var = jnp.mean(yc * yc, axis=-1, keepdims=True)
    yn = yc * lax.rsqrt(var + LN_EPS) * lg_ref[...] + lb_ref[...]
    ob_ref[...] = (yn * jax.nn.sigmoid(yn)).astype(ob_ref.dtype)


def _token_mixers(qkv, sinks, c, gb, gx, conf_dw, conf_dw_b, conf_ln_g, conf_ln_b, sc_dw,
                  seq_len):
    M = c.shape[0]
    bt = CONV_BT
    assert bt == 2 * BLOCK
    blocks_per_seq = seq_len // bt
    k_col = OFF_K // KV_W
    v_col = OFF_V // KV_W
    row = lambda i: (i, 0)
    whole = lambda i: (0, 0)
    prev_blk = lambda i: jnp.maximum(2 * i - 1, 0)
    band_bias = pl.BlockSpec((1, 4 * BLOCK, 4 * BLOCK), lambda i: (0, 0, 0))
    in_specs = [
        pl.BlockSpec(memory_space=pltpu.SMEM),
        pl.BlockSpec((bt, ATTN_W), row),
        pl.BlockSpec((BLOCK, KV_W), lambda i: (prev_blk(i), k_col)),
        pl.BlockSpec((bt, KV_W), lambda i: (i, k_col)),
        pl.BlockSpec((BLOCK, KV_W), lambda i: (prev_blk(i), v_col)),
        pl.BlockSpec((bt, KV_W), lambda i: (i, v_col)),
        pl.BlockSpec((1, 4 * BLOCK, 4 * BLOCK),
                     lambda i: (jnp.where(i % blocks_per_seq == 0, 1, 0), 0, 0)),
        band_bias,
        pl.BlockSpec((bt, CONF_W), row),
        pl.BlockSpec((CONF_HALO, CONF_W), lambda i: (jnp.maximum(i * (bt // CONF_HALO) - 1, 0), 0)),
        pl.BlockSpec((bt, SC_W), row),
        pl.BlockSpec((bt, SC_W), row),
        pl.BlockSpec((SC_HALO, SC_W), lambda i: (jnp.maximum(i * (bt // SC_HALO) - 1, 0), 0)),
        pl.BlockSpec((CONF_K, CONF_W), whole),
        pl.BlockSpec((1, CONF_W), whole),
        pl.BlockSpec((1, CONF_W), whole),
        pl.BlockSpec((1, CONF_W), whole),
        pl.BlockSpec((SC_K, SC_W), whole),
    ]
    est = 2 * (3 * _nbytes((bt, CONF_W), F32) + 2 * _nbytes((bt, CONF_W), BF16)) \
        + 6 * _nbytes((bt + CONF_HALO, CONF_W), F32) \
        + _nbytes((SUBLANE - 1, CONV_SHIFT_ROWS, CONF_W), F32)
    est += 2 * (2 * _nbytes((bt, ATTN_W), BF16) + 3 * _nbytes((bt, KV_W), BF16)
                + 2 * _nbytes((4 * BLOCK, 4 * BLOCK), F32)) \
        + 16 * _nbytes((4 * BLOCK, 4 * BLOCK), F32)
    bias = _attn_bias()
    return pl.pallas_call(
        functools.partial(_token_mixers_body, blocks_per_seq=blocks_per_seq),
        grid=(M // bt,),
        in_specs=in_specs,
        out_specs=[pl.BlockSpec((bt, ATTN_W), row), pl.BlockSpec((bt, CONF_W), row),
                   pl.BlockSpec((bt, SC_W), row)],
        out_shape=[jax.ShapeDtypeStruct((M, ATTN_W), BF16),
                   jax.ShapeDtypeStruct((M, CONF_W), BF16), jax.ShapeDtypeStruct((M, SC_W), BF16)],
        scratch_shapes=[pltpu.VMEM((bt + CONF_HALO, CONF_W), F32),
                        pltpu.VMEM((bt, CONF_W), F32),
                        pltpu.VMEM((bt + SC_HALO, SC_W), F32),
                        pltpu.VMEM((SUBLANE - 1, CONV_SHIFT_ROWS, CONF_W), F32)],
        compiler_params=_compiler_params(1, est), name="token_mixers",
    )(sinks, qkv, qkv, qkv, qkv, qkv, bias, bias, c, c, gb, gx, gx, conf_dw,
      conf_dw_b.reshape(1, -1), conf_ln_g.reshape(1, -1), conf_ln_b.reshape(1, -1), sc_dw)


def _merge_body(x_ref, oa_ref, ob_ref, oc_ref, wg0_ref, wg1_ref, wg2_ref,
                pa_ref, pb_ref, pc_ref, out_ref):
    for r in range(out_ref.shape[0] // DOT_ROWS):
        rows = slice(r * DOT_ROWS, (r + 1) * DOT_ROWS)
        acc = None
        for o_ref, wg_ref, p_ref in ((oa_ref, wg0_ref, pa_ref), (ob_ref, wg1_ref, pb_ref),
                                     (oc_ref, wg2_ref, pc_ref)):
            gate = jax.nn.sigmoid(
                jnp.dot(x_ref[rows, :], wg_ref[...], preferred_element_type=F32))
            term = gate * jnp.dot(o_ref[rows, :], p_ref[...], preferred_element_type=F32)
            acc = term if acc is None else acc + term
        out_ref[rows, :] = acc.astype(out_ref.dtype)


def _merge(xb, oa, ob, oc, w_in_b, pa, pb, pc, layer, *, bm, bn):
    M = xb.shape[0]
    n_blocks = D_MODEL // bn
    lhs_spec = lambda a: pl.BlockSpec((bm, a.shape[1]), lambda m, n: (m, 0))
    gate_spec = lambda i: pl.BlockSpec(
        (None, D_MODEL, bn),
        functools.partial(_col_index, layer=layer, base=(OFF_GATE + i * D_MODEL) // bn,
                          last=n_blocks - 1))
    p_spec = lambda p: pl.BlockSpec((None, p.shape[1], bn), lambda m, n: (layer, 0, n))
    k_total = D_MODEL + ATTN_W + CONF_W + SC_W
    est = 2 * (_nbytes((bm, k_total), BF16) + _nbytes((3 * D_MODEL + k_total - D_MODEL, bn), BF16)
               + _nbytes((bm, bn), BF16)) + 8 * _nbytes((bm, bn), F32)
    return pl.pallas_call(
        _merge_body,
        grid=(M // bm, D_MODEL // bn),
        in_specs=[lhs_spec(xb), lhs_spec(oa), lhs_spec(ob), lhs_spec(oc),
                  gate_spec(0), gate_spec(1), gate_spec(2), p_spec(pa), p_spec(pb), p_spec(pc)],
        out_specs=pl.BlockSpec((bm, bn), lambda m, n: (m, n)),
        out_shape=jax.ShapeDtypeStruct((M, D_MODEL), BF16),
        compiler_params=_compiler_params(2, est), name="gated_merge",
    )(xb, oa, ob, oc, w_in_b, w_in_b, w_in_b, pa, pb, pc)


LN_ROWS = 256
LN_CHUNK = 128
LN_COLS = DOT_COLS


def _layer_norm_step(load_piece, step, n_pieces, g_ref, b_ref, out_refs):
    width = n_pieces * LN_COLS
    for i in range(LN_ROWS // LN_CHUNK):
        rows = pl.ds(pl.multiple_of(step * LN_ROWS + i * LN_CHUNK, LN_CHUNK), LN_CHUNK)
        out_rows = slice(i * LN_CHUNK, (i + 1) * LN_CHUNK)
        total = None
        for j in range(n_pieces):
            s = jnp.sum(load_piece(j, rows), axis=-1, keepdims=True)
            total = s if total is None else total + s
        mu = total / width
        total = None
        for j in range(n_pieces):
            d = load_piece(j, rows) - mu
            s = jnp.sum(d * d, axis=-1, keepdims=True)
            total = s if total is None else total + s
        rstd = lax.rsqrt(total / width + LN_EPS)
        for j in range(n_pieces):
            cols = slice(j * LN_COLS, (j + 1) * LN_COLS)
            out = (load_piece(j, rows) - mu) * rstd * g_ref[:, cols] + b_ref[:, cols]
            for o_ref in out_refs:
                o_ref[out_rows, cols] = out.astype(o_ref.dtype)


def _ln_prefetch_maps(n_main, n_row_blocks):
    def row_blk(m, s):
        return jnp.where(s >= n_main, jnp.minimum(m + 1, n_row_blocks - 1), m)

    def col_blk(s):
        return jnp.where(s >= n_main, 0, s)

    return row_blk, col_blk


def _ln_outputs(n_main, ln_steps, n_rows, width, out_dtypes):
    index = lambda m, s: (m * ln_steps + jnp.maximum(s - n_main, 0), 0)
    specs = [pl.BlockSpec((LN_ROWS, width), index) for _ in out_dtypes]
    shapes = [jax.ShapeDtypeStruct((n_rows, width), dt) for dt in out_dtypes]
    vmem = 2 * sum(_nbytes((LN_ROWS, width), dt) for dt in out_dtypes)
    return specs, shapes, vmem


def _mm_acc_res_ln_body(a_ref, w_ref, res_ref, g_ref, b_ref, *rest, nk, n_res):
    out_refs, acc_ref = rest[:-1], rest[-1]
    k = pl.program_id(1)
    bm, width = acc_ref.shape
    res_w = width // n_res

    @pl.when(k == 0)
    def _():
        acc_ref[...] = jnp.zeros_like(acc_ref)

    @pl.when(k < nk)
    def _():
        for r in range(bm // DOT_ROWS):
            rows = slice(r * DOT_ROWS, (r + 1) * DOT_ROWS)
            for j in range(width // DOT_COLS):
                cols = slice(j * DOT_COLS, (j + 1) * DOT_COLS)
                acc_ref[rows, cols] += jnp.dot(a_ref[rows, :], w_ref[:, cols],
                                               preferred_element_type=F32)

    for j in range(n_res):
        @pl.when(k == j)
        def _(j=j):
            cols = slice(j * res_w, (j + 1) * res_w)
            acc_ref[:, cols] += ALPHA * res_ref[...]

    @pl.when(k >= nk)
    def _():
        _layer_norm_step(lambda j, rows: acc_ref[rows, j * LN_COLS:(j + 1) * LN_COLS],
                         k - nk, width // LN_COLS, g_ref, b_ref, out_refs)


def _mm_acc_res_ln(a, w, layer, res, g, b, out_dtypes, *, bm, bk, name):
    M, K = a.shape
    width = w.shape[2]
    nk = K // bk
    ln_steps = bm // LN_ROWS
    n_res = min(nk, 8)
    res_w = width // n_res
    out_specs, out_shape, out_vmem = _ln_outputs(nk, ln_steps, M, width, out_dtypes)
    est = 2 * (_nbytes((bm, bk), BF16) + _nbytes((bk, width), BF16) + _nbytes((bm, res_w), F32)) \
        + out_vmem + _nbytes((bm, width), F32) + 4 * _nbytes((DOT_ROWS, DOT_COLS), F32)
    row_blk, col_blk = _ln_prefetch_maps(nk, M // bm)
    return pl.pallas_call(
        functools.partial(_mm_acc_res_ln_body, nk=nk, n_res=n_res),
        grid=(M // bm, nk + ln_steps),
        in_specs=[pl.BlockSpec((bm, bk), lambda m, k: (row_blk(m, k), col_blk(k))),
                  pl.BlockSpec((None, bk, width), lambda m, k: (layer, col_blk(k), 0)),
                  pl.BlockSpec((bm, res_w),
                               lambda m, k: (row_blk(m, k), jnp.minimum(col_blk(k), n_res - 1))),
                  pl.BlockSpec((1, width), lambda m, k: (0, 0)),
                  pl.BlockSpec((1, width), lambda m, k: (0, 0))],
        out_specs=out_specs, out_shape=out_shape,
        scratch_shapes=[pltpu.VMEM((bm, width), F32)],
        compiler_params=_compiler_params(2, est), name=name,
    )(a, w, res, g.reshape(1, -1), b.reshape(1, -1))


def _mm_cols_res_ln_body(a_ref, w_ref, res_ref, g_ref, b_ref, *rest, n_cols):
    out_refs, acc_ref = rest[:-1], rest[-1]
    n = pl.program_id(1)
    bm = acc_ref.shape[1]

    @pl.when(n < n_cols)
    def _():
        for r in range(bm // DOT_ROWS):
            rows = slice(r * DOT_ROWS, (r + 1) * DOT_ROWS)
            acc_ref[n, rows, :] = ALPHA * res_ref[rows, :] + jnp.dot(
                a_ref[rows, :], w_ref[...], preferred_element_type=F32)

    @pl.when(n >= n_cols)
    def _():
        _layer_norm_step(lambda j, rows: acc_ref[j, rows, :], n - n_cols, n_cols,
                         g_ref, b_ref, out_refs)


def _mm_cols_res_ln(a, w, layer, res, g, b, out_dtypes, *, bm, name):
    M, K = a.shape
    width = w.shape[2]
    bn = LN_COLS
    n_cols = width // bn
    ln_steps = bm // LN_ROWS
    out_specs, out_shape, out_vmem = _ln_outputs(n_cols, ln_steps, M, width, out_dtypes)
    est = 2 * (_nbytes((bm, K), BF16) + _nbytes((K, bn), BF16) + _nbytes((bm, bn), F32)) \
        + out_vmem + _nbytes((bm, width), F32) + 2 * _nbytes((DOT_ROWS, bn), F32)
    row_blk, col_blk = _ln_prefetch_maps(n_cols, M // bm)
    return pl.pallas_call(
        functools.partial(_mm_cols_res_ln_body, n_cols=n_cols),
        grid=(M // bm, n_cols + ln_steps),
        in_specs=[pl.BlockSpec((bm, K), lambda m, n: (row_blk(m, n), 0)),
                  pl.BlockSpec((None, K, bn), lambda m, n: (layer, 0, col_blk(n))),
                  pl.BlockSpec((bm, bn), lambda m, n: (row_blk(m, n), col_blk(n))),
                  pl.BlockSpec((1, width), lambda m, n: (0, 0)),
                  pl.BlockSpec((1, width), lambda m, n: (0, 0))],
        out_specs=out_specs, out_shape=out_shape,
        scratch_shapes=[pltpu.VMEM((n_cols, bm, bn), F32)],
        compiler_params=_compiler_params(2, est), name=name,
    )(a, w, res, g.reshape(1, -1), b.reshape(1, -1))


CAST_ROWS = 512


def _cast_pad_rows_body(w_ref, o_ref, *, rows_in):
    row = pl.program_id(1) * CAST_ROWS + lax.broadcasted_iota(jnp.int32, w_ref.shape, 0)
    o_ref[...] = jnp.where(row < rows_in, w_ref[...], 0.0).astype(o_ref.dtype)


def _cast_pad_rows(w, rows_out):
    depth, rows_in, cols = w.shape
    spec = pl.BlockSpec((None, CAST_ROWS, cols), lambda l, r: (l, r, 0))
    est = 2 * (_nbytes((CAST_ROWS, cols), F32) + _nbytes((CAST_ROWS, cols), BF16))
    return pl.pallas_call(
        functools.partial(_cast_pad_rows_body, rows_in=rows_in),
        grid=(depth, rows_out // CAST_ROWS),
        in_specs=[spec], out_specs=spec,
        out_shape=jax.ShapeDtypeStruct((depth, rows_out, cols), BF16),
        compiler_params=_compiler_params(2, est), name="cast_pad_rows",
    )(w)


def _layer(l, h32, hbf, seq_len, wts, sinks, conf_dw, conf_dw_b, conf_ln_g, conf_ln_b, sc_dw,
           ln1_g, ln1_b, ln2_g, ln2_b):
    w_in_b, pa_b, pb_b, pc_b, w_out_b, w_ffn_in_b, w_ffn_down_b = wts
    (qkv,) = _proj(hbf, w_in_b, l, (OFF_Q,), _qkv_epilogue, (BF16,), n_cols=QKV_W,
                   out_cols=QKV_W, bm=QKV_TILE[0], bn=QKV_TILE[1], name="qkv_proj")
    c, gb, gx = _proj(hbf, w_in_b, l,
                      (OFF_CONF_VAL, OFF_CONF_GATE, OFF_SC_B, OFF_SC_C, OFF_SC_X),
                      _mixer_epilogue, (F32, F32, F32), n_cols=CONF_W, out_cols=CONF_W,
                      bm=MIXER_TILE[0], bn=MIXER_TILE[1], name="mixer_proj")
    o_a, o_b, o_c = _token_mixers(qkv, sinks, c, gb, gx, conf_dw, conf_dw_b, conf_ln_g,
                                  conf_ln_b, sc_dw, seq_len)
    merged = _merge(hbf, o_a, o_b, o_c, w_in_b, pa_b, pb_b, pc_b, l,
                    bm=MERGE_TILE[0], bn=MERGE_TILE[1])
    x32, xbf = _mm_cols_res_ln(merged, w_out_b, l, h32, ln1_g, ln1_b, (F32, BF16),
                               bm=OUT_PROJ_ROWS, name="out_proj_ln")
    (act,) = _proj(xbf, w_ffn_in_b, l, (0, D_FF), _swiglu_epilogue, (BF16,), n_cols=D_FF,
                   out_cols=D_FF_PAD, bm=FFN_IN_TILE[0], bn=FFN_IN_TILE[1],
                   name="ffn_in_swiglu")
    out_dtypes = (F32,) if l == DEPTH - 1 else (F32, BF16)
    outs = _mm_acc_res_ln(act, w_ffn_down_b, l, x32, ln2_g, ln2_b, out_dtypes,
                          bm=FFN_DOWN_TILE[0], bk=FFN_DOWN_TILE[1], name="ffn_down_ln")
    return outs[0], outs[-1]


def kernel(x, w_in, attn_sinks, conf_dw, conf_dw_b, conf_ln_g, conf_ln_b, sc_dw, w_proj_attn,
           w_proj_conf, w_proj_sc, w_out, ln1_g, ln1_b, w_ffn_in, w_ffn_down, ln2_g, ln2_b):
    batch, seq_len, d_model = x.shape
    h32 = x.reshape(batch * seq_len, d_model)
    hbf = h32.astype(BF16)
    wts = (w_in.astype(BF16), w_proj_attn.astype(BF16), w_proj_conf.astype(BF16),
           w_proj_sc.astype(BF16), w_out.astype(BF16), w_ffn_in.astype(BF16),
           _cast_pad_rows(w_ffn_down, D_FF_PAD))
    for l in range(DEPTH):
        h32, hbf = _layer(l, h32, hbf, seq_len, wts, attn_sinks[l], conf_dw[l], conf_dw_b[l],
                          conf_ln_g[l], conf_ln_b[l], sc_dw[l], ln1_g[l], ln1_b[l],
                          ln2_g[l], ln2_b[l])
    return h32.reshape(batch, seq_len, d_model)
```
